```python
import math
import jax, jax.numpy as jnp
from jax import lax
import numpy as np

D_MODEL = 1024
BATCH = 16
SEQ = 4096
DEPTH = 1
DEC_BATCH = 128
DEC_SEQ = 8
PAST_LEN = 8192
PAGE_SIZE = 128

H_ATTN = 4
HEAD_DIM = 64
DK = 2 * HEAD_DIM
DV = 2 * HEAD_DIM
ATT_W = H_ATTN * DV
CONV_W = D_MODEL - ATT_W
CONV_K = 3
D_FF = 2816
NUM_BUCKETS = 32
MAX_EXACT = NUM_BUCKETS // 2
MAX_DISTANCE = 128
Q_BLOCK = 128
EPS = 1e-6
SCALE = HEAD_DIM ** -0.5
NEG_INF = -1e30

kernel_name = 'hymba_diffattn_shortconv_convffn_step'


def rmsnorm(x, g):
    xf = x.astype(jnp.float32)
    y = xf * lax.rsqrt(jnp.mean(xf * xf, axis=-1, keepdims=True) + EPS) * g.astype(jnp.float32)
    return y.astype(x.dtype)


def lambda_init(layer):
    return 0.8 - 0.6 * math.exp(-0.3 * layer)


def t5_bucket(rel):
    n = jnp.maximum(rel, 0)
    nf = jnp.maximum(n, MAX_EXACT).astype(jnp.float32)
    large = MAX_EXACT + (jnp.log(nf / MAX_EXACT) / math.log(MAX_DISTANCE / MAX_EXACT)
                         * (NUM_BUCKETS - MAX_EXACT)).astype(jnp.int32)
    large = jnp.minimum(large, NUM_BUCKETS - 1)
    return jnp.where(n < MAX_EXACT, n, large)


def diff_attention(q, k, v, q_pos, k_pos, lam, lam_init, rel_bias, subln_g):
    b, tq = q.shape[:2]
    tk = k.shape[1]
    qf = q.reshape(b, tq, H_ATTN, 2, HEAD_DIM).astype(jnp.float32)
    kf = k.reshape(b, tk, H_ATTN, 2, HEAD_DIM).astype(jnp.float32)
    logits = jnp.einsum('bqhcd,bkhcd->bhcqk', qf, kf) * SCALE
    rel = q_pos[:, None] - k_pos[None, :]
    bias = jnp.transpose(rel_bias.astype(jnp.float32)[t5_bucket(rel)], (2, 0, 1))
    logits = jnp.where(rel >= 0, logits + bias[None, :, None], NEG_INF)
    probs = jax.nn.softmax(logits, axis=-1)
    weights = probs[:, :, 0] - lam * probs[:, :, 1]
    out = jnp.einsum('bhqk,bkhe->bqhe', weights, v.astype(jnp.float32))
    out = rmsnorm(out, subln_g) * (1.0 - lam_init)
    return out.astype(q.dtype)


def causal_dwconv(u, prefix, w):
    t = u.shape[1]
    p = jnp.concatenate([prefix.astype(u.dtype), u], axis=1)
    y = sum(w[j].astype(u.dtype) * p[:, j:j + t] for j in range(CONV_K))
    return y, p[:, -(CONV_K - 1):]


def hybrid_layer(x, conv_prefix, ffn_prefix, attend, norm_mix_g, w_in, conv_w, w_out,
                 norm_ffn_g, w_up, ffn_conv_w, w_down):
    b, t = x.shape[:2]
    h = rmsnorm(x, norm_mix_g)
    proj = jnp.einsum('btd,de->bte', h, w_in)
    q, k, v, gate_b, gate_c, u = jnp.split(
        proj, [ATT_W, 2 * ATT_W, 3 * ATT_W, 3 * ATT_W + CONV_W, 3 * ATT_W + 2 * CONV_W], axis=-1)
    q = q.reshape(b, t, H_ATTN, DK)
    k = k.reshape(b, t, H_ATTN, DK)
    v = v.reshape(b, t, H_ATTN, DV)
    attn = attend(q, k, v)
    conv_out, conv_state = causal_dwconv(gate_c * u, conv_prefix, conv_w)
    mix = jnp.concatenate([attn.reshape(b, t, ATT_W), gate_b * conv_out], axis=-1)
    x = x + jnp.einsum('bte,ed->btd', mix, w_out)
    z = jnp.einsum('btd,de->bte', rmsnorm(x, norm_ffn_g), w_up)
    z, ffn_state = causal_dwconv(z, ffn_prefix, ffn_conv_w)
    z_up, z_gate = jnp.split(z, 2, axis=-1)
    x = x + jnp.einsum('btf,fd->btd', jax.nn.silu(z_gate) * z_up, w_down)
    return x, k, v, conv_state, ffn_state


def setup_inputs(seed: int = 0) -> dict:
    key = jax.random.key(seed)
    ks = jax.random.split(key, 24)
    normal = jax.random.normal
    n_pages = PAST_LEN // PAGE_SIZE
    n_used = DEC_BATCH * n_pages
    n_pool = n_used + n_used // 4
    page_table = jax.random.permutation(ks[0], n_pool)[:n_used].reshape(DEC_BATCH, n_pages).astype(jnp.int32)
    return {
        'x_prompt': normal(ks[1], (BATCH, SEQ, D_MODEL), jnp.float32),
        'x_sample': normal(ks[2], (DEC_BATCH, DEC_SEQ, D_MODEL), jnp.float32),
        'cache_k': normal(ks[3], (DEPTH, n_pool, PAGE_SIZE, H_ATTN, DK), jnp.float32),
        'cache_v': normal(ks[4], (DEPTH, n_pool, PAGE_SIZE, H_ATTN, DV), jnp.float32),
        'state_conv': normal(ks[5], (DEPTH, DEC_BATCH, CONV_K - 1, CONV_W), jnp.float32),
        'state_ffn': normal(ks[6], (DEPTH, DEC_BATCH, CONV_K - 1, 2 * D_FF), jnp.float32),
        'page_table': page_table,
        'norm_mix_g': 1.0 + 0.02 * normal(ks[7], (DEPTH, D_MODEL), jnp.float32),
        'w_in': normal(ks[8], (DEPTH, D_MODEL, 3 * ATT_W + 3 * CONV_W), jnp.float32) * D_MODEL ** -0.5,
        'conv_w': normal(ks[9], (DEPTH, CONV_K, CONV_W), jnp.float32) * CONV_K ** -0.5,
        'lambda_q1': 0.1 * normal(ks[10], (DEPTH, HEAD_DIM), jnp.float32),
        'lambda_k1': 0.1 * normal(ks[11], (DEPTH, HEAD_DIM), jnp.float32),
        'lambda_q2': 0.1 * normal(ks[12], (DEPTH, HEAD_DIM), jnp.float32),
        'lambda_k2': 0.1 * normal(ks[13], (DEPTH, HEAD_DIM), jnp.float32),
        'subln_g': 1.0 + 0.02 * normal(ks[14], (DEPTH, DV), jnp.float32),
        'rel_bias': 0.1 * normal(ks[15], (NUM_BUCKETS, H_ATTN), jnp.float32),
        'w_out': normal(ks[16], (DEPTH, D_MODEL, D_MODEL), jnp.float32) * D_MODEL ** -0.5,
        'norm_ffn_g': 1.0 + 0.02 * normal(ks[17], (DEPTH, D_MODEL), jnp.float32),
        'w_up': normal(ks[18], (DEPTH, D_MODEL, 2 * D_FF), jnp.float32) * D_MODEL ** -0.5,
        'ffn_conv_w': normal(ks[19], (DEPTH, CONV_K, 2 * D_FF), jnp.float32) * CONV_K ** -0.5,
        'w_down': normal(ks[20], (DEPTH, D_FF, D_MODEL), jnp.float32) * D_FF ** -0.5,
        'norm_final_g': 1.0 + 0.02 * normal(ks[21], (D_MODEL,), jnp.float32),
    }


def reference(x_prompt, x_sample, cache_k, cache_v, state_conv, state_ffn, page_table,
              norm_mix_g, w_in, conv_w, lambda_q1, lambda_k1, lambda_q2, lambda_k2, subln_g,
              rel_bias, w_out, norm_ffn_g, w_up, ffn_conv_w, w_down, norm_final_g):
    xp, xs = x_prompt, x_sample
    b_p, seq = xp.shape[:2]
    b_s, dec_seq = xs.shape[:2]
    n_pages = page_table.shape[1]
    past_len = n_pages * PAGE_SIZE
    kp_l, vp_l, cp_l, fp_l, ks_l, vs_l, cs_l, fs_l = [], [], [], [], [], [], [], []
    for layer in range(DEPTH):
        lam_init = lambda_init(layer)
        lam = (jnp.exp(jnp.sum(lambda_q1[layer].astype(jnp.float32) * lambda_k1[layer].astype(jnp.float32)))
               - jnp.exp(jnp.sum(lambda_q2[layer].astype(jnp.float32) * lambda_k2[layer].astype(jnp.float32)))
               + lam_init)
        sg = subln_g[layer]

        def prompt_attend(q, k, v):
            k_pos = jnp.arange(seq)

            def one_block(i):
                qb = lax.dynamic_slice_in_dim(q, i * Q_BLOCK, Q_BLOCK, axis=1)
                q_pos = i * Q_BLOCK + jnp.arange(Q_BLOCK)
                return diff_attention(qb, k, v, q_pos, k_pos, lam, lam_init, rel_bias, sg)

            blocks = lax.map(one_block, jnp.arange(seq // Q_BLOCK))
            return jnp.transpose(blocks, (1, 0, 2, 3, 4)).reshape(q.shape[0], seq, H_ATTN, DV)

        def sample_attend(q, k, v):
            k_past = cache_k[layer][page_table].reshape(b_s, past_len, H_ATTN, DK).astype(k.dtype)
            v_past = cache_v[layer][page_table].reshape(b_s, past_len, H_ATTN, DV).astype(v.dtype)
            k_all = jnp.concatenate([k_past, k], axis=1)
            v_all = jnp.concatenate([v_past, v], axis=1)
            q_pos = past_len + jnp.arange(dec_seq)
            k_pos = jnp.arange(past_len + dec_seq)
            return diff_attention(q, k_all, v_all, q_pos, k_pos, lam, lam_init, rel_bias, sg)

        layer_w = (norm_mix_g[layer], w_in[layer], conv_w[layer], w_out[layer],
                   norm_ffn_g[layer], w_up[layer], ffn_conv_w[layer], w_down[layer])
        zero_conv = jnp.zeros((b_p, CONV_K - 1, CONV_W), xp.dtype)
        zero_ffn = jnp.zeros((b_p, CONV_K - 1, 2 * D_FF), xp.dtype)
        xp, kp, vp, cp, fp = hybrid_layer(xp, zero_conv, zero_ffn, prompt_attend, *layer_w)
        xs, ks, vs, cs, fs = hybrid_layer(xs, state_conv[layer], state_ffn[layer], sample_attend, *layer_w)
        kp_l.append(kp); vp_l.append(vp); cp_l.append(cp); fp_l.append(fp)
        ks_l.append(ks); vs_l.append(vs); cs_l.append(cs); fs_l.append(fs)
    y_prompt = rmsnorm(xp, norm_final_g)
    y_sample = rmsnorm(xs, norm_final_g)
    new_k_prompt = jnp.stack(kp_l)
    new_v_prompt = jnp.stack(vp_l)
    new_conv_prompt = jnp.stack(cp_l)
    new_ffn_prompt = jnp.stack(fp_l)
    new_k_sample = jnp.stack(ks_l)
    new_v_sample = jnp.stack(vs_l)
    new_conv_sample = jnp.stack(cs_l)
    new_ffn_sample = jnp.stack(fs_l)
    return (y_prompt, y_sample, new_k_prompt, new_v_prompt, new_conv_prompt, new_ffn_prompt,
            new_k_sample, new_v_sample, new_conv_sample, new_ffn_sample)
```

```python
import functools
import math

import jax
import jax.numpy as jnp
import numpy as np
from jax import lax
from jax.experimental import pallas as pl
from jax.experimental.pallas import tpu as pltpu

D_MODEL = 1024
H_ATTN = 4
HEAD_DIM = 64
DV = 2 * HEAD_DIM
ATT_W = H_ATTN * DV
CONV_W = D_MODEL - ATT_W
CONV_K = 3
D_FF = 2816
NUM_BUCKETS = 32
MAX_EXACT = NUM_BUCKETS // 2
MAX_DISTANCE = 128
PAGE_SIZE = 128
EPS = 1e-6
SCALE = HEAD_DIM ** -0.5
NEG_INF = -1e30
LAM_INIT = 0.8 - 0.6 * math.exp(-0.3 * 0)

LANES = 128
SUBLANES = 8
VMEM_LIMIT = 56 * 1024 * 1024

TM = 512
TMS = 256
FC = 256
NC = D_FF // FC
TA = 256
PAGES_PER_STEP = 8

F32 = jnp.float32
BF16 = jnp.bfloat16


def _bucket_table(n_max):
    n = np.arange(n_max)
    nf = np.maximum(n, MAX_EXACT).astype(np.float64)
    large = MAX_EXACT + (np.log(nf / MAX_EXACT) / math.log(MAX_DISTANCE / MAX_EXACT)
                         * (NUM_BUCKETS - MAX_EXACT)).astype(np.int64)
    large = np.minimum(large, NUM_BUCKETS - 1)
    return np.where(n < MAX_EXACT, n, large).astype(np.int32)


def _rms(x, g):
    return x * lax.rsqrt(jnp.mean(x * x, axis=-1, keepdims=True) + EPS) * g


def _shift_rows_seq(z, carry):
    row = lax.broadcasted_iota(jnp.int32, z.shape, 0)
    c6 = carry[6:7, :]
    c7 = carry[7:8, :]
    z1 = jnp.where(row == 0, c7, pltpu.roll(z, 1, 0))
    z2 = jnp.where(row == 0, c6, jnp.where(row == 1, c7, pltpu.roll(z, 2, 0)))
    return z1, z2


def _shift_rows_groups(z, prefix):
    row = lax.broadcasted_iota(jnp.int32, z.shape, 0) % SUBLANES
    up_one = pltpu.roll(prefix, prefix.shape[0] - 1, 0)
    z1 = jnp.where(row >= 1, pltpu.roll(z, 1, 0), up_one)
    z2 = jnp.where(row >= 2, pltpu.roll(z, 2, 0), prefix)
    return z1, z2


def _conv3(w, z, z1, z2):
    return w[0:1, :] * z2 + w[1:2, :] * z1 + w[2:3, :] * z


def _project(x, g_ref, w_ref):
    h = _rms(x, g_ref[...]).astype(BF16)
    return jnp.dot(h, w_ref[...], preferred_element_type=F32)


def _store_heads(dst_ref, val):
    rows = val.shape[0]
    for hh in range(H_ATTN):
        dst_ref[pl.ds(hh, rows, stride=H_ATTN), :] = val[:, hh * DV:(hh + 1) * DV]


def _inproj_prompt_kernel(x_ref, g_ref, w_ref, cw_ref, q_ref, k4_ref, v4_ref, kb_ref, vt_ref,
                          gated_ref, cst_ref, carry_ref):
    @pl.when(pl.program_id(1) == 0)
    def _():
        carry_ref[...] = jnp.zeros_like(carry_ref)

    proj = _project(x_ref[...], g_ref, w_ref)
    q = proj[:, 0:ATT_W]
    k = proj[:, ATT_W:2 * ATT_W]
    v = proj[:, 2 * ATT_W:3 * ATT_W]
    gate_b = proj[:, 3 * ATT_W:3 * ATT_W + CONV_W]
    gate_c = proj[:, 3 * ATT_W + CONV_W:3 * ATT_W + 2 * CONV_W]
    u = proj[:, 3 * ATT_W + 2 * CONV_W:]
    q_ref[...] = (q * SCALE).astype(BF16)
    kb_ref[...] = k.astype(BF16)
    vt_ref[...] = v.T.astype(BF16)
    _store_heads(k4_ref, k)
    _store_heads(v4_ref, v)
    cu = gate_c * u
    z1, z2 = _shift_rows_seq(cu, carry_ref[...])
    gated_ref[...] = (gate_b * _conv3(cw_ref[...], cu, z1, z2)).astype(BF16)
    rows = cu.shape[0]
    carry_ref[...] = cu[rows - SUBLANES:, :]
    cst_ref[...] = cu[rows - (CONV_K - 1):, :]


def _inproj_sample_kernel(x_ref, g_ref, w_ref, cw_ref, pre_ref, q_ref, k_ref, v_ref, gated_ref,
                          cu_ref):
    proj = _project(x_ref[...], g_ref, w_ref)
    q_ref[...] = proj[:, 0:ATT_W] * SCALE
    k_ref[...] = proj[:, ATT_W:2 * ATT_W]
    v_ref[...] = proj[:, 2 * ATT_W:3 * ATT_W]
    gate_b = proj[:, 3 * ATT_W:3 * ATT_W + CONV_W]
    gate_c = proj[:, 3 * ATT_W + CONV_W:3 * ATT_W + 2 * CONV_W]
    u = proj[:, 3 * ATT_W + 2 * CONV_W:]
    cu = gate_c * u
    z1, z2 = _shift_rows_groups(cu, pre_ref[...])
    gated_ref[...] = (gate_b * _conv3(cw_ref[...], cu, z1, z2)).astype(BF16)
    cu_ref[...] = cu


def _const_spec(shape):
    nd = len(shape)
    return pl.BlockSpec(shape, lambda *_: (0,) * nd, pipeline_mode=pl.Buffered(1))


def _inproj_prompt(x, g, w, cw):
    b, s, d = x.shape
    nt = s // TM
    e = w.shape[1]
    out_shape = (
        jax.ShapeDtypeStruct((b, s, ATT_W), BF16),
        jax.ShapeDtypeStruct((b, s * H_ATTN, DV), F32),
        jax.ShapeDtypeStruct((b, s * H_ATTN, DV), F32),
        jax.ShapeDtypeStruct((b, s, ATT_W), BF16),
        jax.ShapeDtypeStruct((b, ATT_W, s), BF16),
        jax.ShapeDtypeStruct((b, s, CONV_W), BF16),
        jax.ShapeDtypeStruct((b, CONV_K - 1, CONV_W), F32),
    )
    tok = lambda bi, ti: (bi, ti, 0)
    return pl.pallas_call(
        _inproj_prompt_kernel,
        grid=(b, nt),
        in_specs=[
            pl.BlockSpec((None, TM, d), tok),
            _const_spec((1, d)),
            _const_spec((d, e)),
            _const_spec((CONV_K, CONV_W)),
        ],
        out_specs=(
            pl.BlockSpec((None, TM, ATT_W), tok),
            pl.BlockSpec((None, TM * H_ATTN, DV), tok),
            pl.BlockSpec((None, TM * H_ATTN, DV), tok),
            pl.BlockSpec((None, TM, ATT_W), tok),
            pl.BlockSpec((None, ATT_W, TM), lambda bi, ti: (bi, 0, ti)),
            pl.BlockSpec((None, TM, CONV_W), tok),
            pl.BlockSpec((None, CONV_K - 1, CONV_W), lambda bi, ti: (bi, 0, 0)),
        ),
        out_shape=out_shape,
        scratch_shapes=[pltpu.VMEM((SUBLANES, CONV_W), F32)],
        compiler_params=pltpu.CompilerParams(
            dimension_semantics=("arbitrary", "arbitrary"), vmem_limit_bytes=VMEM_LIMIT),
        name="inproj_prompt",
    )(x, g, w, cw)


def _inproj_sample(x, g, w, cw, prefix):
    n, d = x.shape
    e = w.shape[1]
    nt = n // TMS
    tok = lambda ti: (ti, 0)
    out_shape = (
        jax.ShapeDtypeStruct((n, ATT_W), F32),
        jax.ShapeDtypeStruct((n, ATT_W), F32),
        jax.ShapeDtypeStruct((n, ATT_W), F32),
        jax.ShapeDtypeStruct((n, CONV_W), BF16),
        jax.ShapeDtypeStruct((n, CONV_W), F32),
    )
    return pl.pallas_call(
        _inproj_sample_kernel,
        grid=(nt,),
        in_specs=[
            pl.BlockSpec((TMS, d), tok),
            _const_spec((1, d)),
            _const_spec((d, e)),
            _const_spec((CONV_K, CONV_W)),
            pl.BlockSpec((TMS, CONV_W), tok),
        ],
        out_specs=(
            pl.BlockSpec((TMS, ATT_W), tok),
            pl.BlockSpec((TMS, ATT_W), tok),
            pl.BlockSpec((TMS, ATT_W), tok),
            pl.BlockSpec((TMS, CONV_W), tok),
            pl.BlockSpec((TMS, CONV_W), tok),
        ),
        out_shape=out_shape,
        compiler_params=pltpu.CompilerParams(
            dimension_semantics=("arbitrary",), vmem_limit_bytes=VMEM_LIMIT),
        name="inproj_sample",
    )(x, g, w, cw, prefix)


def _lambda(lq1_ref, lk1_ref, lq2_ref, lk2_ref):
    s1 = jnp.sum(lq1_ref[...] * lk1_ref[...], axis=-1, keepdims=True)
    s2 = jnp.sum(lq2_ref[...] * lk2_ref[...], axis=-1, keepdims=True)
    return jnp.exp(s1) - jnp.exp(s2) + LAM_INIT


def _bias_by_code(code, rb_ref):
    val = jnp.zeros(code.shape, F32)
    for bk in range(NUM_BUCKETS - 1):
        for hh in range(H_ATTN):
            val = jnp.where(code == bk * H_ATTN + hh,
                            rb_ref[bk, hh] - rb_ref[NUM_BUCKETS - 1, hh], val)
    return val


def _attn_prompt_kernel(rb_ref, code_ref, lq1_ref, lk1_ref, lq2_ref, lk2_ref, sg_ref,
                        q_ref, k_ref, vt_ref, o_ref, bias_ref, m_ref, acc_ref):
    t = TA
    nb = t // LANES
    i = pl.program_id(1)

    @pl.when((pl.program_id(0) == 0) & (i == 0))
    def _():
        r = lax.broadcasted_iota(jnp.int32, (LANES, LANES), 0)
        c = lax.broadcasted_iota(jnp.int32, (LANES, LANES), 1)
        base = _bias_by_code(code_ref[...], rb_ref)
        for hh in range(H_ATTN):
            rows = jnp.broadcast_to(base[hh:hh + 1, :], (LANES, LANES))
            circ = pltpu.roll(rows, 0, 1, stride=1, stride_axis=0)
            near = jnp.where(c >= r, circ, NEG_INF)
            wrap = jnp.where(c < r, circ, 0.0)
            zero = jnp.zeros((LANES, LANES), F32)
            for kb in range(nb):
                for qb in range(nb):
                    if qb < kb:
                        blk = jnp.full((LANES, LANES), NEG_INF, F32)
                    elif qb == kb:
                        blk = near
                    elif qb == kb + 1:
                        blk = wrap
                    else:
                        blk = zero
                    bias_ref[hh, 0, kb * LANES:(kb + 1) * LANES, qb * LANES:(qb + 1) * LANES] = blk
                    sub = wrap if (kb == nb - 1 and qb == 0) else zero
                    bias_ref[hh, 1, kb * LANES:(kb + 1) * LANES, qb * LANES:(qb + 1) * LANES] = sub

    lam = _lambda(lq1_ref, lk1_ref, lq2_ref, lk2_ref)
    lane = lax.broadcasted_iota(jnp.int32, (t, DV), 1)
    ones = jnp.ones((2 * SUBLANES, t), BF16)

    for hh in range(H_ATTN):
        cols = slice(hh * DV, (hh + 1) * DV)
        qh = q_ref[:, cols]
        q2 = jnp.concatenate([jnp.where(lane < HEAD_DIM, qh, jnp.zeros_like(qh)),
                              jnp.where(lane >= HEAD_DIM, qh, jnp.zeros_like(qh))], axis=0)
        m_ref[...] = jnp.full(m_ref.shape, NEG_INF, F32)
        acc_ref[...] = jnp.zeros_like(acc_ref)

        def tile(j, bias):
            start = pl.multiple_of(j * t, t)
            kj = k_ref[pl.ds(start, t), cols]
            st = lax.dot_general(kj, q2, (((1,), (1,)), ((), ())),
                                 preferred_element_type=F32)
            if bias is not None:
                st = jnp.concatenate([st[:, :t] + bias, st[:, t:] + bias], axis=1)
            m_old = m_ref[...]
            m_new = jnp.maximum(m_old, jnp.max(st, axis=0, keepdims=True))
            alpha = jnp.exp(m_old - m_new)
            p = jnp.exp(st - m_new).astype(BF16)
            vt = jnp.concatenate([vt_ref[cols, pl.ds(start, t)], ones], axis=0)
            pv = jnp.dot(vt, p, preferred_element_type=F32)
            acc_ref[...] = alpha * acc_ref[...] + pv
            m_ref[...] = m_new

        def far(j, carry):
            tile(j, None)
            return carry

        lax.fori_loop(0, jnp.maximum(i - 1, 0), far, 0)

        @pl.when(i >= 1)
        def _():
            tile(i - 1, bias_ref[hh, 1])

        tile(i, bias_ref[hh, 0])

        acc = acc_ref[...]
        inv_l = 1.0 / acc[DV:DV + 1, :]
        o = acc[0:DV, 0:t] * inv_l[:, 0:t] - lam * (acc[0:DV, t:] * inv_l[:, t:])
        o = o * lax.rsqrt(jnp.mean(o * o, axis=0, keepdims=True) + EPS)
        o_ref[:, cols] = ((o.T * sg_ref[...]) * (1.0 - LAM_INIT)).astype(BF16)


def _attn_prompt(q, kb, vt, rel_bias, lq1, lk1, lq2, lk2, sg):
    b, s, _ = q.shape
    t = TA
    code = (_bucket_table(LANES)[None, :] * H_ATTN + np.arange(H_ATTN)[:, None]).astype(np.int32)
    code = np.where(_bucket_table(LANES)[None, :] == NUM_BUCKETS - 1, -1, code)
    smem = pl.BlockSpec(memory_space=pltpu.SMEM)
    small = lambda shape: _const_spec(shape)
    return pl.pallas_call(
        _attn_prompt_kernel,
        grid=(b, s // t),
        in_specs=[
            smem,
            small((H_ATTN, LANES)),
            small((1, HEAD_DIM)), small((1, HEAD_DIM)), small((1, HEAD_DIM)), small((1, HEAD_DIM)),
            small((1, DV)),
            pl.BlockSpec((None, t, ATT_W), lambda bi, qi: (bi, qi, 0)),
            pl.BlockSpec((None, s, ATT_W), lambda bi, qi: (bi, 0, 0)),
            pl.BlockSpec((None, ATT_W, s), lambda bi, qi: (bi, 0, 0)),
        ],
        out_specs=pl.BlockSpec((None, t, ATT_W), lambda bi, qi: (bi, qi, 0)),
        out_shape=jax.ShapeDtypeStruct((b, s, ATT_W), BF16),
        scratch_shapes=[
            pltpu.VMEM((H_ATTN, 2, t, t), F32),
            pltpu.VMEM((1, 2 * t), F32),
            pltpu.VMEM((DV + 2 * SUBLANES, 2 * t), F32),
        ],
        compiler_params=pltpu.CompilerParams(
            dimension_semantics=("arbitrary", "arbitrary"), vmem_limit_bytes=VMEM_LIMIT),
        name="attn_prompt",
    )(rel_bias, jnp.asarray(code), lq1, lk1, lq2, lk2, sg, q, kb, vt)


def _page_rows(ref):
    return jnp.concatenate(
        [ref[pl.ds(hh, PAGE_SIZE, stride=H_ATTN), :] for hh in range(H_ATTN)], axis=1)


def _attn_sample_kernel(pt_ref, rb_ref, code_last_ref, code_new_ref,
                        lq1_ref, lk1_ref, lq2_ref, lk2_ref, sg_ref, q_ref, kn_ref, vn_ref, *rest):
    npg = PAGES_PER_STEP
    k_refs = rest[:npg]
    v_refs = rest[npg:2 * npg]
    o_ref, bias_ref, qbd_ref, kpad_ref, vpad_ref, m_ref, l_ref, acc_ref = rest[2 * npg:]
    del pt_ref
    g = pl.program_id(1)
    n_steps = pl.num_programs(1)
    nq = q_ref.shape[0]
    rows = 2 * H_ATTN * nq

    @pl.when((pl.program_id(0) == 0) & (g == 0))
    def _():
        bias_ref[0] = jnp.zeros((LANES, LANES), F32)
        bias_ref[1] = _bias_by_code(code_last_ref[...], rb_ref)
        code_new = code_new_ref[...]
        bias_ref[2] = jnp.where(code_new == -2, NEG_INF, _bias_by_code(code_new, rb_ref))
        kpad_ref[...] = jnp.zeros_like(kpad_ref)
        vpad_ref[...] = jnp.zeros_like(vpad_ref)
        qbd_ref[...] = jnp.zeros_like(qbd_ref)

    def update(s, v, first):
        m_cur = jnp.max(s, axis=1, keepdims=True)
        if first:
            m_new = m_cur
        else:
            m_old = m_ref[...]
            m_new = jnp.maximum(m_old, m_cur)
            alpha = jnp.exp(m_old - m_new)
        p = jnp.exp(s - m_new)
        l_cur = jnp.sum(p, axis=1, keepdims=True)
        pv = jnp.dot(p, v, preferred_element_type=F32)
        if first:
            l_ref[...] = l_cur
            acc_ref[...] = pv
        else:
            l_ref[...] = alpha * l_ref[...] + l_cur
            acc_ref[...] = alpha * acc_ref[...] + pv
        m_ref[...] = m_new

    def logits_t(kmat):
        return lax.dot_general(kmat, qbd_ref[...], (((1,), (1,)), ((), ())),
                               preferred_element_type=F32)

    @pl.when(g == 0)
    def _():
        q = q_ref[...]
        lane = lax.broadcasted_iota(jnp.int32, q.shape, 1)
        for hh in range(H_ATTN):
            for cc in range(2):
                lo = hh * DV + cc * HEAD_DIM
                r0 = (hh * 2 + cc) * nq
                qbd_ref[r0:r0 + nq, :] = jnp.where((lane >= lo) & (lane < lo + HEAD_DIM), q, 0.0)
        kpad_ref[0:nq, :] = kn_ref[...]
        vpad_ref[0:nq, :] = vn_ref[...]
        st = logits_t(kpad_ref[...]) + bias_ref[2]
        update(st.T[0:rows, :], vpad_ref[...], True)

    last = (g == n_steps - 1).astype(jnp.int32)
    st_pages = [logits_t(_page_rows(k_refs[pi])) for pi in range(npg)]
    st_pages[npg - 1] = st_pages[npg - 1] + bias_ref[last]
    st = jnp.concatenate(st_pages, axis=0)
    v = jnp.concatenate([_page_rows(v_refs[pi]) for pi in range(npg)], axis=0)
    update(st.T[0:rows, :], v, False)

    @pl.when(g == n_steps - 1)
    def _():
        lam = _lambda(lq1_ref, lk1_ref, lq2_ref, lk2_ref)
        o_all = acc_ref[...] * (1.0 / l_ref[...])
        for hh in range(H_ATTN):
            cols = slice(hh * DV, (hh + 1) * DV)
            r0 = hh * 2 * nq
            o = o_all[r0:r0 + nq, cols] - lam * o_all[r0 + nq:r0 + 2 * nq, cols]
            o = o * lax.rsqrt(jnp.mean(o * o, axis=-1, keepdims=True) + EPS)
            o_ref[:, cols] = (o * sg_ref[...]) * (1.0 - LAM_INIT)


def _attn_sample(page_table, q, kn, vn, cache_k, cache_v, rel_bias, lq1, lk1, lq2, lk2, sg):
    bs, n_pages = page_table.shape
    nq = q.shape[0] // bs
    npg = PAGES_PER_STEP
    steps = n_pages // npg
    rows = 2 * H_ATTN * nq
    bkt = _bucket_table(PAGE_SIZE + nq + 1)
    col = np.arange(LANES)
    col_h = col // (2 * nq)
    col_i = col % nq
    key = np.arange(LANES)[:, None]
    rel_last = PAGE_SIZE + col_i[None, :] - key
    code_last = bkt[rel_last] * H_ATTN + col_h[None, :]
    code_last = np.where((col[None, :] >= rows) | (bkt[rel_last] == NUM_BUCKETS - 1), -1, code_last)
    rel_new = col_i[None, :] - key
    code_new = bkt[np.maximum(rel_new, 0)] * H_ATTN + col_h[None, :]
    code_new = np.where((rel_new < 0) | (key >= nq), -2, code_new)
    code_new = np.where(col[None, :] >= rows, -1, code_new)

    smem = pl.BlockSpec(memory_space=pltpu.SMEM)
    small = lambda shape: pl.BlockSpec(shape, lambda bi, gi, pt: (0,) * len(shape))
    per_b = pl.BlockSpec((nq, ATT_W), lambda bi, gi, pt: (bi, 0))

    def page_spec(pi):
        return pl.BlockSpec((None, PAGE_SIZE * H_ATTN, DV),
                            lambda bi, gi, pt: (pt[bi, gi * npg + pi], 0, 0))

    grid_spec = pltpu.PrefetchScalarGridSpec(
        num_scalar_prefetch=1,
        grid=(bs, steps),
        in_specs=[
            smem,
            small((LANES, LANES)), small((LANES, LANES)),
            small((1, HEAD_DIM)), small((1, HEAD_DIM)), small((1, HEAD_DIM)), small((1, HEAD_DIM)),
            small((1, DV)),
            per_b, per_b, per_b,
        ] + [page_spec(pi) for pi in range(npg)] + [page_spec(pi) for pi in range(npg)],
        out_specs=per_b,
        scratch_shapes=[
            pltpu.VMEM((3, LANES, LANES), F32),
            pltpu.VMEM((LANES, ATT_W), F32),
            pltpu.VMEM((LANES, ATT_W), F32),
            pltpu.VMEM((LANES, ATT_W), F32),
            pltpu.VMEM((rows, 1), F32),
            pltpu.VMEM((rows, 1), F32),
            pltpu.VMEM((rows, ATT_W), F32),
        ],
    )
    return pl.pallas_call(
        _attn_sample_kernel,
        grid_spec=grid_spec,
        out_shape=jax.ShapeDtypeStruct((bs * nq, ATT_W), F32),
        compiler_params=pltpu.CompilerParams(
            dimension_semantics=("arbitrary", "arbitrary"), vmem_limit_bytes=VMEM_LIMIT),
        name="attn_sample",
    )(page_table, rel_bias, jnp.asarray(code_last.astype(np.int32)),
      jnp.asarray(code_new.astype(np.int32)), lq1, lk1, lq2, lk2, sg, q, kn, vn,
      *([cache_k] * npg), *([cache_v] * npg))


def _ffn_core(x, attn, gated, wout_ref, g2_ref, wup_ref, cw_ref, wdn_ref, gf_ref, y_ref,
              acc_ref, h2_ref, shift, keep):
    mix = jnp.concatenate([attn.astype(BF16), gated], axis=1)
    x1 = x + jnp.dot(mix, wout_ref[...], preferred_element_type=F32)
    h2_ref[...] = _rms(x1, g2_ref[...]).astype(BF16)
    acc_ref[...] = x1

    def chunk(c, carry):
        zr = jnp.dot(h2_ref[...], wup_ref[c], preferred_element_type=F32)
        z1, z2 = shift(c, zr)
        z = _conv3(cw_ref[c], zr, z1, z2)
        z_up = z[:, :FC]
        z_gate = z[:, FC:]
        act = (z_gate * (1.0 / (1.0 + jnp.exp(-z_gate)))) * z_up
        acc_ref[...] += jnp.dot(act.astype(BF16), wdn_ref[c], preferred_element_type=F32)
        keep(c, zr)
        return carry

    lax.fori_loop(0, NC, chunk, 0)
    y_ref[...] = _rms(acc_ref[...], gf_ref[...])


def _ffn_prompt_kernel(x_ref, attn_ref, gated_ref, wout_ref, g2_ref, wup_ref, cw_ref, wdn_ref,
                       gf_ref, y_ref, fst_ref, carry_ref, acc_ref, h2_ref):
    @pl.when(pl.program_id(1) == 0)
    def _():
        carry_ref[...] = jnp.zeros_like(carry_ref)

    def shift(c, zr):
        return _shift_rows_seq(zr, carry_ref[c])

    def keep(c, zr):
        rows = zr.shape[0]
        carry_ref[c] = zr[rows - SUBLANES:, :]
        fst_ref[c] = zr[rows - (CONV_K - 1):, :]

    _ffn_core(x_ref[...], attn_ref[...], gated_ref[...], wout_ref, g2_ref, wup_ref, cw_ref,
              wdn_ref, gf_ref, y_ref, acc_ref, h2_ref, shift, keep)


def _ffn_sample_kernel(x_ref, attn_ref, gated_ref, wout_ref, g2_ref, wup_ref, cw_ref, wdn_ref,
                       gf_ref, pre_ref, y_ref, zr_ref, acc_ref, h2_ref):
    def shift(c, zr):
        return _shift_rows_groups(zr, pre_ref[c])

    def keep(c, zr):
        zr_ref[c] = zr

    _ffn_core(x_ref[...], attn_ref[...], gated_ref[...], wout_ref, g2_ref, wup_ref, cw_ref,
              wdn_ref, gf_ref, y_ref, acc_ref, h2_ref, shift, keep)


def _ffn_weight_specs():
    return [
        _const_spec((D_MODEL, D_MODEL)),
        _const_spec((1, D_MODEL)),
        _const_spec((NC, D_MODEL, 2 * FC)),
        _const_spec((NC, CONV_K, 2 * FC)),
        _const_spec((NC, FC, D_MODEL)),
        _const_spec((1, D_MODEL)),
    ]


def _ffn_prompt(x, attn, gated, wout, g2, wup, cw, wdn, gf):
    b, s, d = x.shape
    tok = lambda bi, ti: (bi, ti, 0)
    return pl.pallas_call(
        _ffn_prompt_kernel,
        grid=(b, s // TM),
        in_specs=[
            pl.BlockSpec((None, TM, d), tok),
            pl.BlockSpec((None, TM, ATT_W), tok),
            pl.BlockSpec((None, TM, CONV_W), tok),
        ] + _ffn_weight_specs(),
        out_specs=(
            pl.BlockSpec((None, TM, d), tok),
            pl.BlockSpec((None, NC, CONV_K - 1, 2 * FC), lambda bi, ti: (bi, 0, 0, 0)),
        ),
        out_shape=(
            jax.ShapeDtypeStruct((b, s, d), F32),
            jax.ShapeDtypeStruct((b, NC, CONV_K - 1, 2 * FC), F32),
        ),
        scratch_shapes=[
            pltpu.VMEM((NC, SUBLANES, 2 * FC), F32),
            pltpu.VMEM((TM, d), F32),
            pltpu.VMEM((TM, d), BF16),
        ],
        compiler_params=pltpu.CompilerParams(
            dimension_semantics=("arbitrary", "arbitrary"), vmem_limit_bytes=VMEM_LIMIT),
        name="ffn_prompt",
    )(x, attn, gated, wout, g2, wup, cw, wdn, gf)


def _ffn_sample(x, attn, gated, wout, g2, wup, cw, wdn, gf, prefix):
    n, d = x.shape
    tok = lambda ti: (ti, 0)
    return pl.pallas_call(
        _ffn_sample_kernel,
        grid=(n // TMS,),
        in_specs=[
            pl.BlockSpec((TMS, d), tok),
            pl.BlockSpec((TMS, ATT_W), tok),
            pl.BlockSpec((TMS, CONV_W), tok),
        ] + _ffn_weight_specs() + [
            pl.BlockSpec((NC, TMS, 2 * FC), lambda ti: (0, ti, 0)),
        ],
        out_specs=(
            pl.BlockSpec((TMS, d), tok),
            pl.BlockSpec((NC, TMS, 2 * FC), lambda ti: (0, ti, 0)),
        ),
        out_shape=(
            jax.ShapeDtypeStruct((n, d), F32),
            jax.ShapeDtypeStruct((NC, n, 2 * FC), F32),
        ),
        scratch_shapes=[
            pltpu.VMEM((TMS, d), F32),
            pltpu.VMEM((TMS, d), BF16),
        ],
        compiler_params=pltpu.CompilerParams(
            dimension_semantics=("arbitrary",), vmem_limit_bytes=VMEM_LIMIT),
        name="ffn_sample",
    )(x, attn, gated, wout, g2, wup, cw, wdn, gf, prefix)


def _chunk_cols(a):
    lead = a.shape[:-1]
    a = a.reshape(lead + (2, NC, FC))
    a = jnp.moveaxis(a, -2, 0)
    return a.reshape((NC,) + lead + (2 * FC,))


def _unchunk_cols(a):
    lead = a.shape[1:-1]
    a = a.reshape((NC,) + lead + (2, FC))
    a = jnp.moveaxis(a, 0, -2)
    return a.reshape(lead + (2 * D_FF,))


def _group_prefix(state):
    bsz, k, c = state.shape
    pad = jnp.zeros((bsz, SUBLANES - k, c), state.dtype)
    return jnp.concatenate([state, pad], axis=1).reshape(bsz * SUBLANES, c)


def kernel(x_prompt, x_sample, cache_k, cache_v, state_conv, state_ffn, page_table, norm_mix_g, w_in, conv_w, lambda_q1, lambda_k1, lambda_q2, lambda_k2, subln_g, rel_bias, w_out, norm_ffn_g, w_up, ffn_conv_w, w_down, norm_final_g):
    depth = w_in.shape[0]
    assert depth == 1
    bp, seq, d = x_prompt.shape
    bs, dec, _ = x_sample.shape
    assert dec == SUBLANES and seq % TM == 0 and seq % TA == 0 and (bs * dec) % TMS == 0
    n_pool = cache_k.shape[1]
    layer = 0

    g1 = norm_mix_g[layer].reshape(1, d)
    g2 = norm_ffn_g[layer].reshape(1, d)
    gf = norm_final_g.reshape(1, d)
    w_in_b = w_in[layer].astype(BF16)
    w_out_b = w_out[layer].astype(BF16)
    w_up_b = _chunk_cols(w_up[layer].astype(BF16))
    w_dn_b = w_down[layer].astype(BF16).reshape(NC, FC, d)
    cw = conv_w[layer]
    fcw = _chunk_cols(ffn_conv_w[layer])
    lq1 = lambda_q1[layer].reshape(1, HEAD_DIM)
    lk1 = lambda_k1[layer].reshape(1, HEAD_DIM)
    lq2 = lambda_q2[layer].reshape(1, HEAD_DIM)
    lk2 = lambda_k2[layer].reshape(1, HEAD_DIM)
    sg = subln_g[layer].reshape(1, DV)

    q_p, k4_p, v4_p, kb_p, vt_p, gated_p, cst_p = _inproj_prompt(x_prompt, g1, w_in_b, cw)
    attn_p = _attn_prompt(q_p, kb_p, vt_p, rel_bias, lq1, lk1, lq2, lk2, sg)
    y_p, fst_p = _ffn_prompt(x_prompt, attn_p, gated_p, w_out_b, g2, w_up_b, fcw, w_dn_b, gf)

    xs = x_sample.reshape(bs * dec, d)
    q_s, k_s, v_s, gated_s, cu_s = _inproj_sample(xs, g1, w_in_b, cw, _group_prefix(state_conv[layer]))
    ck = cache_k.reshape(depth * n_pool, PAGE_SIZE * H_ATTN, DV)
    cv = cache_v.reshape(depth * n_pool, PAGE_SIZE * H_ATTN, DV)
    attn_s = _attn_sample(page_table, q_s, k_s, v_s, ck, cv, rel_bias, lq1, lk1, lq2, lk2, sg)
    pre_ffn = _chunk_cols(_group_prefix(state_ffn[layer]))
    y_s, zr_s = _ffn_sample(xs, attn_s, gated_s, w_out_b, g2, w_up_b, fcw, w_dn_b, gf, pre_ffn)

    keep = CONV_K - 1
    y_prompt = y_p
    y_sample = y_s.reshape(bs, dec, d)
    new_k_prompt = k4_p.reshape(1, bp, seq, H_ATTN, DV)
    new_v_prompt = v4_p.reshape(1, bp, seq, H_ATTN, DV)
    new_conv_prompt = cst_p[None]
    new_ffn_prompt = _unchunk_cols(jnp.moveaxis(fst_p, 1, 0))[None]
    new_k_sample = k_s.reshape(1, bs, dec, H_ATTN, DV)
    new_v_sample = v_s.reshape(1, bs, dec, H_ATTN, DV)
    new_conv_sample = cu_s.reshape(bs, dec, CONV_W)[:, dec - keep:, :][None]
    new_ffn_sample = _unchunk_cols(zr_s).reshape(bs, dec, 2 * D_FF)[:, dec - keep:, :][None]
    return (y_prompt, y_sample, new_k_prompt, new_v_prompt, new_conv_prompt, new_ffn_prompt,
            new_k_sample, new_v_sample, new_conv_sample, new_ffn_sample)
```

```python
import functools
import math

import jax
import jax.numpy as jnp
import numpy as np
from jax import lax
from jax.experimental import pallas as pl
from jax.experimental.pallas import tpu as pltpu

D_MODEL = 1024
H_ATTN = 4
HEAD_DIM = 64
DV = 2 * HEAD_DIM
ATT_W = H_ATTN * DV
CONV_W = D_MODEL - ATT_W
CONV_K = 3
D_FF = 2816
NUM_BUCKETS = 32
MAX_EXACT = NUM_BUCKETS // 2
MAX_DISTANCE = 128
PAGE_SIZE = 128
EPS = 1e-6
SCALE = HEAD_DIM ** -0.5
NEG_INF = -1e30
LAM_INIT = 0.8 - 0.6 * math.exp(-0.3 * 0)

LANES = 128
SUBLANES = 8
VMEM_LIMIT = 56 * 1024 * 1024

TM = 512
TMS = 256
FC = 256
NC = D_FF // FC
TA = 256
PAGES_PER_STEP = 16

F32 = jnp.float32
BF16 = jnp.bfloat16


def _bucket_table(n_max):
    n = np.arange(n_max)
    nf = np.maximum(n, MAX_EXACT).astype(np.float64)
    large = MAX_EXACT + (np.log(nf / MAX_EXACT) / math.log(MAX_DISTANCE / MAX_EXACT)
                         * (NUM_BUCKETS - MAX_EXACT)).astype(np.int64)
    large = np.minimum(large, NUM_BUCKETS - 1)
    return np.where(n < MAX_EXACT, n, large).astype(np.int32)


def _rms(x, g):
    return x * lax.rsqrt(jnp.mean(x * x, axis=-1, keepdims=True) + EPS) * g


def _shift_rows_seq(z, carry):
    row = lax.broadcasted_iota(jnp.int32, z.shape, 0)
    c6 = carry[6:7, :]
    c7 = carry[7:8, :]
    z1 = jnp.where(row == 0, c7, pltpu.roll(z, 1, 0))
    z2 = jnp.where(row == 0, c6, jnp.where(row == 1, c7, pltpu.roll(z, 2, 0)))
    return z1, z2


def _shift_rows_groups(z, prefix):
    row = lax.broadcasted_iota(jnp.int32, z.shape, 0) % SUBLANES
    up_one = pltpu.roll(prefix, prefix.shape[0] - 1, 0)
    z1 = jnp.where(row >= 1, pltpu.roll(z, 1, 0), up_one)
    z2 = jnp.where(row >= 2, pltpu.roll(z, 2, 0), prefix)
    return z1, z2


def _conv3(w, z, z1, z2):
    return w[0:1, :] * z2 + w[1:2, :] * z1 + w[2:3, :] * z


def _project(x, g_ref, w_ref):
    h = _rms(x, g_ref[...]).astype(BF16)
    return jnp.dot(h, w_ref[...], preferred_element_type=F32)


def _store_heads(dst_ref, val):
    rows = val.shape[0]
    for hh in range(H_ATTN):
        dst_ref[pl.ds(hh, rows, stride=H_ATTN), :] = val[:, hh * DV:(hh + 1) * DV]


def _inproj_prompt_kernel(x_ref, g_ref, w_ref, cw_ref, q_ref, k4_ref, v4_ref, kb_ref, vt_ref,
                          gated_ref, cst_ref, carry_ref):
    @pl.when(pl.program_id(1) == 0)
    def _():
        carry_ref[...] = jnp.zeros_like(carry_ref)

    proj = _project(x_ref[...], g_ref, w_ref)
    q = proj[:, 0:ATT_W]
    k = proj[:, ATT_W:2 * ATT_W]
    v = proj[:, 2 * ATT_W:3 * ATT_W]
    gate_b = proj[:, 3 * ATT_W:3 * ATT_W + CONV_W]
    gate_c = proj[:, 3 * ATT_W + CONV_W:3 * ATT_W + 2 * CONV_W]
    u = proj[:, 3 * ATT_W + 2 * CONV_W:]
    q_ref[...] = (q * SCALE).astype(BF16)
    kb_ref[...] = k.astype(BF16)
    vt_ref[...] = v.T.astype(BF16)
    _store_heads(k4_ref, k)
    _store_heads(v4_ref, v)
    cu = gate_c * u
    z1, z2 = _shift_rows_seq(cu, carry_ref[...])
    gated_ref[...] = (gate_b * _conv3(cw_ref[...], cu, z1, z2)).astype(BF16)
    rows = cu.shape[0]
    carry_ref[...] = cu[rows - SUBLANES:, :]
    cst_ref[...] = cu[rows - (CONV_K - 1):, :]


def _inproj_sample_kernel(x_ref, g_ref, w_ref, cw_ref, pre_ref, q_ref, k_ref, v_ref, gated_ref,
                          cu_ref):
    proj = _project(x_ref[...], g_ref, w_ref)
    q_ref[...] = proj[:, 0:ATT_W] * SCALE
    k_ref[...] = proj[:, ATT_W:2 * ATT_W]
    v_ref[...] = proj[:, 2 * ATT_W:3 * ATT_W]
    gate_b = proj[:, 3 * ATT_W:3 * ATT_W + CONV_W]
    gate_c = proj[:, 3 * ATT_W + CONV_W:3 * ATT_W + 2 * CONV_W]
    u = proj[:, 3 * ATT_W + 2 * CONV_W:]
    cu = gate_c * u
    z1, z2 = _shift_rows_groups(cu, pre_ref[...])
    gated_ref[...] = (gate_b * _conv3(cw_ref[...], cu, z1, z2)).astype(BF16)
    cu_ref[...] = cu


def _const_spec(shape):
    nd = len(shape)
    return pl.BlockSpec(shape, lambda *_: (0,) * nd, pipeline_mode=pl.Buffered(1))


def _inproj_prompt(x, g, w, cw):
    b, s, d = x.shape
    nt = s // TM
    e = w.shape[1]
    out_shape = (
        jax.ShapeDtypeStruct((b, s, ATT_W), BF16),
        jax.ShapeDtypeStruct((b, s * H_ATTN, DV), F32),
        jax.ShapeDtypeStruct((b, s * H_ATTN, DV), F32),
        jax.ShapeDtypeStruct((b, s, ATT_W), BF16),
        jax.ShapeDtypeStruct((b, ATT_W, s), BF16),
        jax.ShapeDtypeStruct((b, s, CONV_W), BF16),
        jax.ShapeDtypeStruct((b, CONV_K - 1, CONV_W), F32),
    )
    tok = lambda bi, ti: (bi, ti, 0)
    return pl.pallas_call(
        _inproj_prompt_kernel,
        grid=(b, nt),
        in_specs=[
            pl.BlockSpec((None, TM, d), tok),
            _const_spec((1, d)),
            _const_spec((d, e)),
            _const_spec((CONV_K, CONV_W)),
        ],
        out_specs=(
            pl.BlockSpec((None, TM, ATT_W), tok),
            pl.BlockSpec((None, TM * H_ATTN, DV), tok),
            pl.BlockSpec((None, TM * H_ATTN, DV), tok),
            pl.BlockSpec((None, TM, ATT_W), tok),
            pl.BlockSpec((None, ATT_W, TM), lambda bi, ti: (bi, 0, ti)),
            pl.BlockSpec((None, TM, CONV_W), tok),
            pl.BlockSpec((None, CONV_K - 1, CONV_W), lambda bi, ti: (bi, 0, 0)),
        ),
        out_shape=out_shape,
        scratch_shapes=[pltpu.VMEM((SUBLANES, CONV_W), F32)],
        compiler_params=pltpu.CompilerParams(
            dimension_semantics=("arbitrary", "arbitrary"), vmem_limit_bytes=VMEM_LIMIT),
        name="inproj_prompt",
    )(x, g, w, cw)


def _inproj_sample(x, g, w, cw, prefix):
    n, d = x.shape
    e = w.shape[1]
    nt = n // TMS
    tok = lambda ti: (ti, 0)
    out_shape = (
        jax.ShapeDtypeStruct((n, ATT_W), F32),
        jax.ShapeDtypeStruct((n, ATT_W), F32),
        jax.ShapeDtypeStruct((n, ATT_W), F32),
        jax.ShapeDtypeStruct((n, CONV_W), BF16),
        jax.ShapeDtypeStruct((n, CONV_W), F32),
    )
    return pl.pallas_call(
        _inproj_sample_kernel,
        grid=(nt,),
        in_specs=[
            pl.BlockSpec((TMS, d), tok),
            _const_spec((1, d)),
            _const_spec((d, e)),
            _const_spec((CONV_K, CONV_W)),
            pl.BlockSpec((TMS, CONV_W), tok),
        ],
        out_specs=(
            pl.BlockSpec((TMS, ATT_W), tok),
            pl.BlockSpec((TMS, ATT_W), tok),
            pl.BlockSpec((TMS, ATT_W), tok),
            pl.BlockSpec((TMS, CONV_W), tok),
            pl.BlockSpec((TMS, CONV_W), tok),
        ),
        out_shape=out_shape,
        compiler_params=pltpu.CompilerParams(
            dimension_semantics=("arbitrary",), vmem_limit_bytes=VMEM_LIMIT),
        name="inproj_sample",
    )(x, g, w, cw, prefix)


def _lambda(lq1_ref, lk1_ref, lq2_ref, lk2_ref):
    s1 = jnp.sum(lq1_ref[...] * lk1_ref[...], axis=-1, keepdims=True)
    s2 = jnp.sum(lq2_ref[...] * lk2_ref[...], axis=-1, keepdims=True)
    return jnp.exp(s1) - jnp.exp(s2) + LAM_INIT


def _bias_by_code(code, rb_ref):
    val = jnp.zeros(code.shape, F32)
    for bk in range(NUM_BUCKETS - 1):
        for hh in range(H_ATTN):
            val = jnp.where(code == bk * H_ATTN + hh,
                            rb_ref[bk, hh] - rb_ref[NUM_BUCKETS - 1, hh], val)
    return val


def _attn_prompt_kernel(rb_ref, code_ref, lq1_ref, lk1_ref, lq2_ref, lk2_ref, sg_ref,
                        q_ref, k_ref, vt_ref, o_ref, bias_ref, q2_ref, sa_ref, sb_ref, m_ref, acc_ref):
    t = TA
    nb = t // LANES
    i = pl.program_id(1)

    @pl.when((pl.program_id(0) == 0) & (i == 0))
    def _():
        r = lax.broadcasted_iota(jnp.int32, (LANES, LANES), 0)
        c = lax.broadcasted_iota(jnp.int32, (LANES, LANES), 1)
        base = _bias_by_code(code_ref[...], rb_ref)
        for hh in range(H_ATTN):
            rows = jnp.broadcast_to(base[hh:hh + 1, :], (LANES, LANES))
            circ = pltpu.roll(rows, 0, 1, stride=1, stride_axis=0)
            near = jnp.where(c >= r, circ, NEG_INF)
            wrap = jnp.where(c < r, circ, 0.0)
            zero = jnp.zeros((LANES, LANES), F32)
            for kb in range(nb):
                for qb in range(nb):
                    if qb < kb:
                        blk = jnp.full((LANES, LANES), NEG_INF, F32)
                    elif qb == kb:
                        blk = near
                    elif qb == kb + 1:
                        blk = wrap
                    else:
                        blk = zero
                    bias_ref[hh, 0, kb * LANES:(kb + 1) * LANES, qb * LANES:(qb + 1) * LANES] = blk
                    sub = wrap if (kb == nb - 1 and qb == 0) else zero
                    bias_ref[hh, 1, kb * LANES:(kb + 1) * LANES, qb * LANES:(qb + 1) * LANES] = sub

    lam = _lambda(lq1_ref, lk1_ref, lq2_ref, lk2_ref)
    lane = lax.broadcasted_iota(jnp.int32, (t, DV), 1)
    ones = jnp.ones((2 * SUBLANES, t), BF16)

    for hh in range(H_ATTN):
        qh = q_ref[:, hh * DV:(hh + 1) * DV]
        q2_ref[hh] = jnp.concatenate([jnp.where(lane < HEAD_DIM, qh, jnp.zeros_like(qh)),
                                      jnp.where(lane >= HEAD_DIM, qh, jnp.zeros_like(qh))], axis=0)
    m_ref[...] = jnp.full(m_ref.shape, NEG_INF, F32)
    acc_ref[...] = jnp.zeros_like(acc_ref)

    def logits_t(hh, j):
        start = pl.multiple_of(j * t, t)
        kj = k_ref[pl.ds(start, t), hh * DV:(hh + 1) * DV]
        return lax.dot_general(kj, q2_ref[hh], (((1,), (1,)), ((), ())),
                               preferred_element_type=F32)

    def tile(j, cur_ref, nxt_ref, kind, prefetch):
        start = pl.multiple_of(j * t, t)
        for hh in range(H_ATTN):
            cols = slice(hh * DV, (hh + 1) * DV)
            if prefetch:
                nxt_ref[hh] = logits_t(hh, j + 1)
            st = cur_ref[hh]
            if kind is not None:
                bias = bias_ref[hh, kind]
                st = jnp.concatenate([st[:, :t] + bias, st[:, t:] + bias], axis=1)
            m_old = m_ref[hh]
            m_new = jnp.maximum(m_old, jnp.max(st, axis=0, keepdims=True))
            alpha = jnp.exp(m_old - m_new)
            p = jnp.exp(st - m_new).astype(BF16)
            vt = jnp.concatenate([vt_ref[cols, pl.ds(start, t)], ones], axis=0)
            pv = jnp.dot(vt, p, preferred_element_type=F32)
            acc_ref[hh] = alpha * acc_ref[hh] + pv
            m_ref[hh] = m_new

    for hh in range(H_ATTN):
        sa_ref[hh] = logits_t(hh, 0)

    n_pairs = jnp.maximum(i - 1, 0) // 2

    def far_pair(jj, carry):
        tile(2 * jj, sa_ref, sb_ref, None, True)
        tile(2 * jj + 1, sb_ref, sa_ref, None, True)
        return carry

    lax.fori_loop(0, n_pairs, far_pair, 0)

    @pl.when(i == 0)
    def _():
        tile(i, sa_ref, sb_ref, 0, False)

    @pl.when(i % 2 == 1)
    def _():
        tile(i - 1, sa_ref, sb_ref, 1, True)
        tile(i, sb_ref, sa_ref, 0, False)

    @pl.when((i % 2 == 0) & (i >= 2))
    def _():
        tile(i - 2, sa_ref, sb_ref, None, True)
        tile(i - 1, sb_ref, sa_ref, 1, True)
        tile(i, sa_ref, sb_ref, 0, False)

    for hh in range(H_ATTN):
        acc = acc_ref[hh]
        inv_l = 1.0 / acc[DV:DV + 1, :]
        o = acc[0:DV, 0:t] * inv_l[:, 0:t] - lam * (acc[0:DV, t:] * inv_l[:, t:])
        o = o * lax.rsqrt(jnp.mean(o * o, axis=0, keepdims=True) + EPS)
        o_ref[:, hh * DV:(hh + 1) * DV] = ((o.T * sg_ref[...]) * (1.0 - LAM_INIT)).astype(BF16)


def _attn_prompt(q, kb, vt, rel_bias, lq1, lk1, lq2, lk2, sg):
    b, s, _ = q.shape
    t = TA
    code = (_bucket_table(LANES)[None, :] * H_ATTN + np.arange(H_ATTN)[:, None]).astype(np.int32)
    code = np.where(_bucket_table(LANES)[None, :] == NUM_BUCKETS - 1, -1, code)
    smem = pl.BlockSpec(memory_space=pltpu.SMEM)
    small = lambda shape: _const_spec(shape)
    return pl.pallas_call(
        _attn_prompt_kernel,
        grid=(b, s // t),
        in_specs=[
            smem,
            small((H_ATTN, LANES)),
            small((1, HEAD_DIM)), small((1, HEAD_DIM)), small((1, HEAD_DIM)), small((1, HEAD_DIM)),
            small((1, DV)),
            pl.BlockSpec((None, t, ATT_W), lambda bi, qi: (bi, qi, 0)),
            pl.BlockSpec((None, s, ATT_W), lambda bi, qi: (bi, 0, 0)),
            pl.BlockSpec((None, ATT_W, s), lambda bi, qi: (bi, 0, 0)),
        ],
        out_specs=pl.BlockSpec((None, t, ATT_W), lambda bi, qi: (bi, qi, 0)),
        out_shape=jax.ShapeDtypeStruct((b, s, ATT_W), BF16),
        scratch_shapes=[
            pltpu.VMEM((H_ATTN, 2, t, t), F32),
            pltpu.VMEM((H_ATTN, 2 * t, DV), BF16),
            pltpu.VMEM((H_ATTN, t, 2 * t), F32),
            pltpu.VMEM((H_ATTN, t, 2 * t), F32),
            pltpu.VMEM((H_ATTN, 1, 2 * t), F32),
            pltpu.VMEM((H_ATTN, DV + 2 * SUBLANES, 2 * t), F32),
        ],
        compiler_params=pltpu.CompilerParams(
            dimension_semantics=("arbitrary", "arbitrary"), vmem_limit_bytes=VMEM_LIMIT),
        name="attn_prompt",
    )(rel_bias, jnp.asarray(code), lq1, lk1, lq2, lk2, sg, q, kb, vt)


def _page_rows(ref):
    return jnp.concatenate(
        [ref[pl.ds(hh, PAGE_SIZE, stride=H_ATTN), :] for hh in range(H_ATTN)], axis=1)


def _attn_sample_kernel(pt_ref, rb_ref, code_last_ref, code_new_ref,
                        lq1_ref, lk1_ref, lq2_ref, lk2_ref, sg_ref, q_ref, kn_ref, vn_ref, *rest):
    npg = PAGES_PER_STEP
    k_refs = rest[:npg]
    v_refs = rest[npg:2 * npg]
    o_ref, bias_ref, qbd_ref, kpad_ref, vpad_ref, m_ref, l_ref, acc_ref = rest[2 * npg:]
    del pt_ref
    g = pl.program_id(1)
    n_steps = pl.num_programs(1)
    nq = q_ref.shape[0]
    rows = 2 * H_ATTN * nq

    @pl.when((pl.program_id(0) == 0) & (g == 0))
    def _():
        bias_ref[0] = jnp.zeros((LANES, LANES), F32)
        bias_ref[1] = _bias_by_code(code_last_ref[...], rb_ref)
        code_new = code_new_ref[...]
        bias_ref[2] = jnp.where(code_new == -2, NEG_INF, _bias_by_code(code_new, rb_ref))
        kpad_ref[...] = jnp.zeros_like(kpad_ref)
        vpad_ref[...] = jnp.zeros_like(vpad_ref)
        qbd_ref[...] = jnp.zeros_like(qbd_ref)

    def update(s, v, first):
        m_cur = jnp.max(s, axis=1, keepdims=True)
        if first:
            m_new = m_cur
        else:
            m_old = m_ref[...]
            m_new = jnp.maximum(m_old, m_cur)
            alpha = jnp.exp(m_old - m_new)
        p = jnp.exp(s - m_new)
        l_cur = jnp.sum(p, axis=1, keepdims=True)
        pv = jnp.dot(p, v, preferred_element_type=F32)
        if first:
            l_ref[...] = l_cur
            acc_ref[...] = pv
        else:
            l_ref[...] = alpha * l_ref[...] + l_cur
            acc_ref[...] = alpha * acc_ref[...] + pv
        m_ref[...] = m_new

    def logits_t(kmat):
        return lax.dot_general(kmat, qbd_ref[...], (((1,), (1,)), ((), ())),
                               preferred_element_type=F32)

    @pl.when(g == 0)
    def _():
        q = q_ref[...]
        lane = lax.broadcasted_iota(jnp.int32, q.shape, 1)
        for hh in range(H_ATTN):
            for cc in range(2):
                lo = hh * DV + cc * HEAD_DIM
                r0 = (hh * 2 + cc) * nq
                qbd_ref[r0:r0 + nq, :] = jnp.where((lane >= lo) & (lane < lo + HEAD_DIM), q, 0.0)
        kpad_ref[0:nq, :] = kn_ref[...]
        vpad_ref[0:nq, :] = vn_ref[...]
        st = logits_t(kpad_ref[...]) + bias_ref[2]
        update(st.T[0:rows, :], vpad_ref[...], True)

    last = (g == n_steps - 1).astype(jnp.int32)
    st_pages = [logits_t(_page_rows(k_refs[pi])) for pi in range(npg)]
    st_pages[npg - 1] = st_pages[npg - 1] + bias_ref[last]
    st = jnp.concatenate(st_pages, axis=0)
    v = jnp.concatenate([_page_rows(v_refs[pi]) for pi in range(npg)], axis=0)
    update(st.T[0:rows, :], v, False)

    @pl.when(g == n_steps - 1)
    def _():
        lam = _lambda(lq1_ref, lk1_ref, lq2_ref, lk2_ref)
        o_all = acc_ref[...] * (1.0 / l_ref[...])
        for hh in range(H_ATTN):
            cols = slice(hh * DV, (hh + 1) * DV)
            r0 = hh * 2 * nq
            o = o_all[r0:r0 + nq, cols] - lam * o_all[r0 + nq:r0 + 2 * nq, cols]
            o = o * lax.rsqrt(jnp.mean(o * o, axis=-1, keepdims=True) + EPS)
            o_ref[:, cols] = (o * sg_ref[...]) * (1.0 - LAM_INIT)


def _attn_sample(page_table, q, kn, vn, cache_k, cache_v, rel_bias, lq1, lk1, lq2, lk2, sg):
    bs, n_pages = page_table.shape
    nq = q.shape[0] // bs
    npg = PAGES_PER_STEP
    steps = n_pages // npg
    rows = 2 * H_ATTN * nq
    bkt = _bucket_table(PAGE_SIZE + nq + 1)
    col = np.arange(LANES)
    col_h = col // (2 * nq)
    col_i = col % nq
    key = np.arange(LANES)[:, None]
    rel_last = PAGE_SIZE + col_i[None, :] - key
    code_last = bkt[rel_last] * H_ATTN + col_h[None, :]
    code_last = np.where((col[None, :] >= rows) | (bkt[rel_last] == NUM_BUCKETS - 1), -1, code_last)
    rel_new = col_i[None, :] - key
    code_new = bkt[np.maximum(rel_new, 0)] * H_ATTN + col_h[None, :]
    code_new = np.where((rel_new < 0) | (key >= nq), -2, code_new)
    code_new = np.where(col[None, :] >= rows, -1, code_new)

    smem = pl.BlockSpec(memory_space=pltpu.SMEM)
    small = lambda shape: pl.BlockSpec(shape, lambda bi, gi, pt: (0,) * len(shape))
    per_b = pl.BlockSpec((nq, ATT_W), lambda bi, gi, pt: (bi, 0))

    def page_spec(pi):
        return pl.BlockSpec((None, PAGE_SIZE * H_ATTN, DV),
                            lambda bi, gi, pt: (pt[bi, gi * npg + pi], 0, 0))

    grid_spec = pltpu.PrefetchScalarGridSpec(
        num_scalar_prefetch=1,
        grid=(bs, steps),
        in_specs=[
            smem,
            small((LANES, LANES)), small((LANES, LANES)),
            small((1, HEAD_DIM)), small((1, HEAD_DIM)), small((1, HEAD_DIM)), small((1, HEAD_DIM)),
            small((1, DV)),
            per_b, per_b, per_b,
        ] + [page_spec(pi) for pi in range(npg)] + [page_spec(pi) for pi in range(npg)],
        out_specs=per_b,
        scratch_shapes=[
            pltpu.VMEM((3, LANES, LANES), F32),
            pltpu.VMEM((LANES, ATT_W), F32),
            pltpu.VMEM((LANES, ATT_W), F32),
            pltpu.VMEM((LANES, ATT_W), F32),
            pltpu.VMEM((rows, 1), F32),
            pltpu.VMEM((rows, 1), F32),
            pltpu.VMEM((rows, ATT_W), F32),
        ],
    )
    return pl.pallas_call(
        _attn_sample_kernel,
        grid_spec=grid_spec,
        out_shape=jax.ShapeDtypeStruct((bs * nq, ATT_W), F32),
        compiler_params=pltpu.CompilerParams(
            dimension_semantics=("arbitrary", "arbitrary"), vmem_limit_bytes=VMEM_LIMIT),
        name="attn_sample",
    )(page_table, rel_bias, jnp.asarray(code_last.astype(np.int32)),
      jnp.asarray(code_new.astype(np.int32)), lq1, lk1, lq2, lk2, sg, q, kn, vn,
      *([cache_k] * npg), *([cache_v] * npg))


def _ffn_core(x, attn, gated, wout_ref, g2_ref, wup_ref, cw_ref, wdn_ref, gf_ref, y_ref,
              acc_ref, h2_ref, zr_ref, act_ref, shift, keep):
    mix = jnp.concatenate([attn.astype(BF16), gated], axis=1)
    x1 = x + jnp.dot(mix, wout_ref[...], preferred_element_type=F32)
    h2_ref[...] = _rms(x1, g2_ref[...]).astype(BF16)
    acc_ref[...] = x1

    def up(c):
        zr_ref[c % 2] = jnp.dot(h2_ref[...], wup_ref[c], preferred_element_type=F32)

    def down(c):
        acc_ref[...] += jnp.dot(act_ref[c % 2], wdn_ref[c], preferred_element_type=F32)

    def gate(c):
        zr = zr_ref[c % 2]
        z1, z2 = shift(c, zr)
        z = _conv3(cw_ref[c], zr, z1, z2)
        z_up = z[:, :FC]
        z_gate = z[:, FC:]
        act_ref[c % 2] = ((z_gate * (1.0 / (1.0 + jnp.exp(-z_gate)))) * z_up).astype(BF16)
        keep(c, zr)

    up(0)
    for c in range(NC):
        if c + 1 < NC:
            up(c + 1)
        if c >= 1:
            down(c - 1)
        gate(c)
    down(NC - 1)
    y_ref[...] = _rms(acc_ref[...], gf_ref[...])


def _ffn_prompt_kernel(x_ref, attn_ref, gated_ref, wout_ref, g2_ref, wup_ref, cw_ref, wdn_ref,
                       gf_ref, y_ref, fst_ref, carry_ref, acc_ref, h2_ref, zbuf_ref, act_ref):
    @pl.when(pl.program_id(1) == 0)
    def _():
        carry_ref[...] = jnp.zeros_like(carry_ref)

    def shift(c, zr):
        return _shift_rows_seq(zr, carry_ref[c])

    def keep(c, zr):
        rows = zr.shape[0]
        carry_ref[c] = zr[rows - SUBLANES:, :]
        fst_ref[c] = zr[rows - (CONV_K - 1):, :]

    _ffn_core(x_ref[...], attn_ref[...], gated_ref[...], wout_ref, g2_ref, wup_ref, cw_ref,
              wdn_ref, gf_ref, y_ref, acc_ref, h2_ref, zbuf_ref, act_ref, shift, keep)


def _ffn_sample_kernel(x_ref, attn_ref, gated_ref, wout_ref, g2_ref, wup_ref, cw_ref, wdn_ref,
                       gf_ref, pre_ref, y_ref, zr_ref, acc_ref, h2_ref, zbuf_ref, act_ref):
    def shift(c, zr):
        return _shift_rows_groups(zr, pre_ref[c])

    def keep(c, zr):
        zr_ref[c] = zr

    _ffn_core(x_ref[...], attn_ref[...], gated_ref[...], wout_ref, g2_ref, wup_ref, cw_ref,
              wdn_ref, gf_ref, y_ref, acc_ref, h2_ref, zbuf_ref, act_ref, shift, keep)


def _ffn_weight_specs():
    return [
        _const_spec((D_MODEL, D_MODEL)),
        _const_spec((1, D_MODEL)),
        _const_spec((NC, D_MODEL, 2 * FC)),
        _const_spec((NC, CONV_K, 2 * FC)),
        _const_spec((NC, FC, D_MODEL)),
        _const_spec((1, D_MODEL)),
    ]


def _ffn_prompt(x, attn, gated, wout, g2, wup, cw, wdn, gf):
    b, s, d = x.shape
    tok = lambda bi, ti: (bi, ti, 0)
    return pl.pallas_call(
        _ffn_prompt_kernel,
        grid=(b, s // TM),
        in_specs=[
            pl.BlockSpec((None, TM, d), tok),
            pl.BlockSpec((None, TM, ATT_W), tok),
            pl.BlockSpec((None, TM, CONV_W), tok),
        ] + _ffn_weight_specs(),
        out_specs=(
            pl.BlockSpec((None, TM, d), tok),
            pl.BlockSpec((None, NC, CONV_K - 1, 2 * FC), lambda bi, ti: (bi, 0, 0, 0)),
        ),
        out_shape=(
            jax.ShapeDtypeStruct((b, s, d), F32),
            jax.ShapeDtypeStruct((b, NC, CONV_K - 1, 2 * FC), F32),
        ),
        scratch_shapes=[
            pltpu.VMEM((NC, SUBLANES, 2 * FC), F32),
            pltpu.VMEM((TM, d), F32),
            pltpu.VMEM((TM, d), BF16),
            pltpu.VMEM((2, TM, 2 * FC), F32),
            pltpu.VMEM((2, TM, FC), BF16),
        ],
        compiler_params=pltpu.CompilerParams(
            dimension_semantics=("arbitrary", "arbitrary"), vmem_limit_bytes=VMEM_LIMIT),
        name="ffn_prompt",
    )(x, attn, gated, wout, g2, wup, cw, wdn, gf)


def _ffn_sample(x, attn, gated, wout, g2, wup, cw, wdn, gf, prefix):
    n, d = x.shape
    tok = lambda ti: (ti, 0)
    return pl.pallas_call(
        _ffn_sample_kernel,
        grid=(n // TMS,),
        in_specs=[
            pl.BlockSpec((TMS, d), tok),
            pl.BlockSpec((TMS, ATT_W), tok),
            pl.BlockSpec((TMS, CONV_W), tok),
        ] + _ffn_weight_specs() + [
            pl.BlockSpec((NC, TMS, 2 * FC), lambda ti: (0, ti, 0)),
        ],
        out_specs=(
            pl.BlockSpec((TMS, d), tok),
            pl.BlockSpec((NC, TMS, 2 * FC), lambda ti: (0, ti, 0)),
        ),
        out_shape=(
            jax.ShapeDtypeStruct((n, d), F32),
            jax.ShapeDtypeStruct((NC, n, 2 * FC), F32),
        ),
        scratch_shapes=[
            pltpu.VMEM((TMS, d), F32),
            pltpu.VMEM((TMS, d), BF16),
            pltpu.VMEM((2, TMS, 2 * FC), F32),
            pltpu.VMEM((2, TMS, FC), BF16),
        ],
        compiler_params=pltpu.CompilerParams(
            dimension_semantics=("arbitrary",), vmem_limit_bytes=VMEM_LIMIT),
        name="ffn_sample",
    )(x, attn, gated, wout, g2, wup, cw, wdn, gf, prefix)


def _chunk_cols(a):
    lead = a.shape[:-1]
    a = a.reshape(lead + (2, NC, FC))
    a = jnp.moveaxis(a, -2, 0)
    return a.reshape((NC,) + lead + (2 * FC,))


def _unchunk_cols(a):
    lead = a.shape[1:-1]
    a = a.reshape((NC,) + lead + (2, FC))
    a = jnp.moveaxis(a, 0, -2)
    return a.reshape(lead + (2 * D_FF,))


def _group_prefix(state):
    bsz, k, c = state.shape
    pad = jnp.zeros((bsz, SUBLANES - k, c), state.dtype)
    return jnp.concatenate([state, pad], axis=1).reshape(bsz * SUBLANES, c)


def kernel(x_prompt, x_sample, cache_k, cache_v, state_conv, state_ffn, page_table, norm_mix_g, w_in, conv_w, lambda_q1, lambda_k1, lambda_q2, lambda_k2, subln_g, rel_bias, w_out, norm_ffn_g, w_up, ffn_conv_w, w_down, norm_final_g):
    depth = w_in.shape[0]
    assert depth == 1
    bp, seq, d = x_prompt.shape
    bs, dec, _ = x_sample.shape
    assert dec == SUBLANES and seq % TM == 0 and seq % TA == 0 and (bs * dec) % TMS == 0
    n_pool = cache_k.shape[1]
    layer = 0

    g1 = norm_mix_g[layer].reshape(1, d)
    g2 = norm_ffn_g[layer].reshape(1, d)
    gf = norm_final_g.reshape(1, d)
    w_in_b = w_in[layer].astype(BF16)
    w_out_b = w_out[layer].astype(BF16)
    w_up_b = _chunk_cols(w_up[layer].astype(BF16))
    w_dn_b = w_down[layer].astype(BF16).reshape(NC, FC, d)
    cw = conv_w[layer]
    fcw = _chunk_cols(ffn_conv_w[layer])
    lq1 = lambda_q1[layer].reshape(1, HEAD_DIM)
    lk1 = lambda_k1[layer].reshape(1, HEAD_DIM)
    lq2 = lambda_q2[layer].reshape(1, HEAD_DIM)
    lk2 = lambda_k2[layer].reshape(1, HEAD_DIM)
    sg = subln_g[layer].reshape(1, DV)

    q_p, k4_p, v4_p, kb_p, vt_p, gated_p, cst_p = _inproj_prompt(x_prompt, g1, w_in_b, cw)
    attn_p = _attn_prompt(q_p, kb_p, vt_p, rel_bias, lq1, lk1, lq2, lk2, sg)
    y_p, fst_p = _ffn_prompt(x_prompt, attn_p, gated_p, w_out_b, g2, w_up_b, fcw, w_dn_b, gf)

    xs = x_sample.reshape(bs * dec, d)
    q_s, k_s, v_s, gated_s, cu_s = _inproj_sample(xs, g1, w_in_b, cw, _group_prefix(state_conv[layer]))
    ck = cache_k.reshape(depth * n_pool, PAGE_SIZE * H_ATTN, DV)
    cv = cache_v.reshape(depth * n_pool, PAGE_SIZE * H_ATTN, DV)
    attn_s = _attn_sample(page_table, q_s, k_s, v_s, ck, cv, rel_bias, lq1, lk1, lq2, lk2, sg)
    pre_ffn = _chunk_cols(_group_prefix(state_ffn[layer]))
    y_s, zr_s = _ffn_sample(xs, attn_s, gated_s, w_out_b, g2, w_up_b, fcw, w_dn_b, gf, pre_ffn)

    keep = CONV_K - 1
    y_prompt = y_p
    y_sample = y_s.reshape(bs, dec, d)
    new_k_prompt = k4_p.reshape(1, bp, seq, H_ATTN, DV)
    new_v_prompt = v4_p.reshape(1, bp, seq, H_ATTN, DV)
    new_conv_prompt = cst_p[None]
    new_ffn_prompt = _unchunk_cols(jnp.moveaxis(fst_p, 1, 0))[None]
    new_k_sample = k_s.reshape(1, bs, dec, H_ATTN, DV)
    new_v_sample = v_s.reshape(1, bs, dec, H_ATTN, DV)
    new_conv_sample = cu_s.reshape(bs, dec, CONV_W)[:, dec - keep:, :][None]
    new_ffn_sample = _unchunk_cols(zr_s).reshape(bs, dec, 2 * D_FF)[:, dec - keep:, :][None]
    return (y_prompt, y_sample, new_k_prompt, new_v_prompt, new_conv_prompt, new_ffn_prompt,
            new_k_sample, new_v_sample, new_conv_sample, new_ffn_sample)
```

```python
import functools
import math

import jax
import jax.numpy as jnp
import numpy as np
from jax import lax
from jax.experimental import pallas as pl
from jax.experimental.pallas import tpu as pltpu

D_MODEL = 1024
H_ATTN = 4
HEAD_DIM = 64
DV = 2 * HEAD_DIM
ATT_W = H_ATTN * DV
CONV_W = D_MODEL - ATT_W
CONV_K = 3
D_FF = 2816
NUM_BUCKETS = 32
MAX_EXACT = NUM_BUCKETS // 2
MAX_DISTANCE = 128
PAGE_SIZE = 128
EPS = 1e-6
SCALE = HEAD_DIM ** -0.5
LOG2E = math.log2(math.e)
NEG_INF = -1e30
LAM_INIT = 0.8 - 0.6 * math.exp(-0.3 * 0)

LANES = 128
SUBLANES = 8
VMEM_LIMIT = 56 * 1024 * 1024

TM = 512
TMS = 256
FC = 256
NC = D_FF // FC
GATE_ROWS = 64
ZR_SLOTS = 3
ACT_SLOTS = 2
TA = 256
PAGES_PER_STEP = 16

F32 = jnp.float32
BF16 = jnp.bfloat16


def _bucket_table(n_max):
    n = np.arange(n_max)
    nf = np.maximum(n, MAX_EXACT).astype(np.float64)
    large = MAX_EXACT + (np.log(nf / MAX_EXACT) / math.log(MAX_DISTANCE / MAX_EXACT)
                         * (NUM_BUCKETS - MAX_EXACT)).astype(np.int64)
    large = np.minimum(large, NUM_BUCKETS - 1)
    return np.where(n < MAX_EXACT, n, large).astype(np.int32)


def _rms(x, g):
    return x * lax.rsqrt(jnp.mean(x * x, axis=-1, keepdims=True) + EPS) * g


def _shift_rows_seq(z, carry):
    row = lax.broadcasted_iota(jnp.int32, z.shape, 0)
    c6 = carry[6:7, :]
    c7 = carry[7:8, :]
    z1 = jnp.where(row == 0, c7, pltpu.roll(z, 1, 0))
    z2 = jnp.where(row == 0, c6, jnp.where(row == 1, c7, pltpu.roll(z, 2, 0)))
    return z1, z2


def _shift_rows_groups(z, prefix):
    row = lax.broadcasted_iota(jnp.int32, z.shape, 0) % SUBLANES
    up_one = pltpu.roll(prefix, prefix.shape[0] - 1, 0)
    z1 = jnp.where(row >= 1, pltpu.roll(z, 1, 0), up_one)
    z2 = jnp.where(row >= 2, pltpu.roll(z, 2, 0), prefix)
    return z1, z2


def _conv3(w, z, z1, z2):
    return w[0:1, :] * z2 + w[1:2, :] * z1 + w[2:3, :] * z


def _project(x, g_ref, w_ref):
    h = _rms(x, g_ref[...]).astype(BF16)
    return jnp.dot(h, w_ref[...], preferred_element_type=F32)


def _store_heads(dst_ref, val):
    rows = val.shape[0]
    for hh in range(H_ATTN):
        dst_ref[pl.ds(hh, rows, stride=H_ATTN), :] = val[:, hh * DV:(hh + 1) * DV]


def _inproj_prompt_kernel(x_ref, g_ref, w_ref, cw_ref, q_ref, k4_ref, v4_ref, kb_ref, vt_ref,
                          gated_ref, cst_ref, carry_ref):
    @pl.when(pl.program_id(1) == 0)
    def _():
        carry_ref[...] = jnp.zeros_like(carry_ref)

    proj = _project(x_ref[...], g_ref, w_ref)
    q = proj[:, 0:ATT_W]
    k = proj[:, ATT_W:2 * ATT_W]
    v = proj[:, 2 * ATT_W:3 * ATT_W]
    gate_b = proj[:, 3 * ATT_W:3 * ATT_W + CONV_W]
    gate_c = proj[:, 3 * ATT_W + CONV_W:3 * ATT_W + 2 * CONV_W]
    u = proj[:, 3 * ATT_W + 2 * CONV_W:]
    q_ref[...] = (q * (SCALE * LOG2E)).astype(BF16)
    kb_ref[...] = k.astype(BF16)
    vt_ref[...] = v.T.astype(BF16)
    _store_heads(k4_ref, k)
    _store_heads(v4_ref, v)
    cu = gate_c * u
    z1, z2 = _shift_rows_seq(cu, carry_ref[...])
    gated_ref[...] = (gate_b * _conv3(cw_ref[...], cu, z1, z2)).astype(BF16)
    rows = cu.shape[0]
    carry_ref[...] = cu[rows - SUBLANES:, :]
    cst_ref[...] = cu[rows - (CONV_K - 1):, :]


def _inproj_sample_kernel(x_ref, g_ref, w_ref, cw_ref, pre_ref, q_ref, k_ref, v_ref, gated_ref,
                          cu_ref):
    proj = _project(x_ref[...], g_ref, w_ref)
    q_ref[...] = proj[:, 0:ATT_W] * SCALE
    k_ref[...] = proj[:, ATT_W:2 * ATT_W]
    v_ref[...] = proj[:, 2 * ATT_W:3 * ATT_W]
    gate_b = proj[:, 3 * ATT_W:3 * ATT_W + CONV_W]
    gate_c = proj[:, 3 * ATT_W + CONV_W:3 * ATT_W + 2 * CONV_W]
    u = proj[:, 3 * ATT_W + 2 * CONV_W:]
    cu = gate_c * u
    z1, z2 = _shift_rows_groups(cu, pre_ref[...])
    gated_ref[...] = (gate_b * _conv3(cw_ref[...], cu, z1, z2)).astype(BF16)
    cu_ref[...] = cu


def _const_spec(shape):
    nd = len(shape)
    return pl.BlockSpec(shape, lambda *_: (0,) * nd, pipeline_mode=pl.Buffered(1))


def _inproj_prompt(x, g, w, cw):
    b, s, d = x.shape
    nt = s // TM
    e = w.shape[1]
    out_shape = (
        jax.ShapeDtypeStruct((b, s, ATT_W), BF16),
        jax.ShapeDtypeStruct((b, s * H_ATTN, DV), F32),
        jax.ShapeDtypeStruct((b, s * H_ATTN, DV), F32),
        jax.ShapeDtypeStruct((b, s, ATT_W), BF16),
        jax.ShapeDtypeStruct((b, ATT_W, s), BF16),
        jax.ShapeDtypeStruct((b, s, CONV_W), BF16),
        jax.ShapeDtypeStruct((b, CONV_K - 1, CONV_W), F32),
    )
    tok = lambda bi, ti: (bi, ti, 0)
    return pl.pallas_call(
        _inproj_prompt_kernel,
        grid=(b, nt),
        in_specs=[
            pl.BlockSpec((None, TM, d), tok),
            _const_spec((1, d)),
            _const_spec((d, e)),
            _const_spec((CONV_K, CONV_W)),
        ],
        out_specs=(
            pl.BlockSpec((None, TM, ATT_W), tok),
            pl.BlockSpec((None, TM * H_ATTN, DV), tok),
            pl.BlockSpec((None, TM * H_ATTN, DV), tok),
            pl.BlockSpec((None, TM, ATT_W), tok),
            pl.BlockSpec((None, ATT_W, TM), lambda bi, ti: (bi, 0, ti)),
            pl.BlockSpec((None, TM, CONV_W), tok),
            pl.BlockSpec((None, CONV_K - 1, CONV_W), lambda bi, ti: (bi, 0, 0)),
        ),
        out_shape=out_shape,
        scratch_shapes=[pltpu.VMEM((SUBLANES, CONV_W), F32)],
        compiler_params=pltpu.CompilerParams(
            dimension_semantics=("arbitrary", "arbitrary"), vmem_limit_bytes=VMEM_LIMIT),
        name="inproj_prompt",
    )(x, g, w, cw)


def _inproj_sample(x, g, w, cw, prefix):
    n, d = x.shape
    e = w.shape[1]
    nt = n // TMS
    tok = lambda ti: (ti, 0)
    out_shape = (
        jax.ShapeDtypeStruct((n, ATT_W), F32),
        jax.ShapeDtypeStruct((n, ATT_W), F32),
        jax.ShapeDtypeStruct((n, ATT_W), F32),
        jax.ShapeDtypeStruct((n, CONV_W), BF16),
        jax.ShapeDtypeStruct((n, CONV_W), F32),
    )
    return pl.pallas_call(
        _inproj_sample_kernel,
        grid=(nt,),
        in_specs=[
            pl.BlockSpec((TMS, d), tok),
            _const_spec((1, d)),
            _const_spec((d, e)),
            _const_spec((CONV_K, CONV_W)),
            pl.BlockSpec((TMS, CONV_W), tok),
        ],
        out_specs=(
            pl.BlockSpec((TMS, ATT_W), tok),
            pl.BlockSpec((TMS, ATT_W), tok),
            pl.BlockSpec((TMS, ATT_W), tok),
            pl.BlockSpec((TMS, CONV_W), tok),
            pl.BlockSpec((TMS, CONV_W), tok),
        ),
        out_shape=out_shape,
        compiler_params=pltpu.CompilerParams(
            dimension_semantics=("arbitrary",), vmem_limit_bytes=VMEM_LIMIT),
        name="inproj_sample",
    )(x, g, w, cw, prefix)


def _lambda(lq1_ref, lk1_ref, lq2_ref, lk2_ref):
    s1 = jnp.sum(lq1_ref[...] * lk1_ref[...], axis=-1, keepdims=True)
    s2 = jnp.sum(lq2_ref[...] * lk2_ref[...], axis=-1, keepdims=True)
    return jnp.exp(s1) - jnp.exp(s2) + LAM_INIT


def _bias_by_code(code, rb_ref, scale=1.0):
    val = jnp.zeros(code.shape, F32)
    for bk in range(NUM_BUCKETS - 1):
        for hh in range(H_ATTN):
            val = jnp.where(code == bk * H_ATTN + hh,
                            (rb_ref[bk, hh] - rb_ref[NUM_BUCKETS - 1, hh]) * scale, val)
    return val


def _attn_prompt_kernel(rb_ref, code_ref, lq1_ref, lk1_ref, lq2_ref, lk2_ref, sg_ref,
                        q_ref, k_ref, vt_ref, o_ref, bias_ref, q2_ref, sa_ref, sb_ref, m_ref, acc_ref):
    t = TA
    nb = t // LANES
    i = pl.program_id(1)

    @pl.when((pl.program_id(0) == 0) & (i == 0))
    def _():
        r = lax.broadcasted_iota(jnp.int32, (LANES, LANES), 0)
        c = lax.broadcasted_iota(jnp.int32, (LANES, LANES), 1)
        base = _bias_by_code(code_ref[...], rb_ref, LOG2E)
        for hh in range(H_ATTN):
            rows = jnp.broadcast_to(base[hh:hh + 1, :], (LANES, LANES))
            circ = pltpu.roll(rows, 0, 1, stride=1, stride_axis=0)
            near = jnp.where(c >= r, circ, NEG_INF)
            wrap = jnp.where(c < r, circ, 0.0)
            zero = jnp.zeros((LANES, LANES), F32)
            for kb in range(nb):
                for qb in range(nb):
                    if qb < kb:
                        blk = jnp.full((LANES, LANES), NEG_INF, F32)
                    elif qb == kb:
                        blk = near
                    elif qb == kb + 1:
                        blk = wrap
                    else:
                        blk = zero
                    bias_ref[hh, 0, kb * LANES:(kb + 1) * LANES, qb * LANES:(qb + 1) * LANES] = blk
                    sub = wrap if (kb == nb - 1 and qb == 0) else zero
                    bias_ref[hh, 1, kb * LANES:(kb + 1) * LANES, qb * LANES:(qb + 1) * LANES] = sub

    lam = _lambda(lq1_ref, lk1_ref, lq2_ref, lk2_ref)
    lane = lax.broadcasted_iota(jnp.int32, (t, DV), 1)
    ones = jnp.ones((2 * SUBLANES, t), BF16)

    for hh in range(H_ATTN):
        qh = q_ref[:, hh * DV:(hh + 1) * DV]
        q2_ref[hh] = jnp.concatenate([jnp.where(lane < HEAD_DIM, qh, jnp.zeros_like(qh)),
                                      jnp.where(lane >= HEAD_DIM, qh, jnp.zeros_like(qh))], axis=0)
    m_ref[...] = jnp.full(m_ref.shape, NEG_INF, F32)
    acc_ref[...] = jnp.zeros_like(acc_ref)

    def logits_t(hh, j):
        start = pl.multiple_of(j * t, t)
        kj = k_ref[pl.ds(start, t), hh * DV:(hh + 1) * DV]
        return lax.dot_general(kj, q2_ref[hh], (((1,), (1,)), ((), ())),
                               preferred_element_type=F32)

    def tile(j, cur_ref, nxt_ref, kind, prefetch):
        start = pl.multiple_of(j * t, t)
        for hh in range(H_ATTN):
            cols = slice(hh * DV, (hh + 1) * DV)
            if prefetch:
                nxt_ref[hh] = logits_t(hh, j + 1)
            st = cur_ref[hh]
            if kind is not None:
                bias = bias_ref[hh, kind]
                st = jnp.concatenate([st[:, :t] + bias, st[:, t:] + bias], axis=1)
            m_old = m_ref[hh]
            m_new = jnp.maximum(m_old, jnp.max(st, axis=0, keepdims=True))
            alpha = jnp.exp2(m_old - m_new)
            p = jnp.exp2(st - m_new).astype(BF16)
            vt = jnp.concatenate([vt_ref[cols, pl.ds(start, t)], ones], axis=0)
            pv = jnp.dot(vt, p, preferred_element_type=F32)
            acc_ref[hh] = alpha * acc_ref[hh] + pv
            m_ref[hh] = m_new

    for hh in range(H_ATTN):
        sa_ref[hh] = logits_t(hh, 0)

    n_pairs = jnp.maximum(i - 1, 0) // 2

    def far_pair(jj, carry):
        tile(2 * jj, sa_ref, sb_ref, None, True)
        tile(2 * jj + 1, sb_ref, sa_ref, None, True)
        return carry

    lax.fori_loop(0, n_pairs, far_pair, 0)

    @pl.when(i == 0)
    def _():
        tile(i, sa_ref, sb_ref, 0, False)

    @pl.when(i % 2 == 1)
    def _():
        tile(i - 1, sa_ref, sb_ref, 1, True)
        tile(i, sb_ref, sa_ref, 0, False)

    @pl.when((i % 2 == 0) & (i >= 2))
    def _():
        tile(i - 2, sa_ref, sb_ref, None, True)
        tile(i - 1, sb_ref, sa_ref, 1, True)
        tile(i, sa_ref, sb_ref, 0, False)

    for hh in range(H_ATTN):
        acc = acc_ref[hh]
        inv_l = 1.0 / acc[DV:DV + 1, :]
        o = acc[0:DV, 0:t] * inv_l[:, 0:t] - lam * (acc[0:DV, t:] * inv_l[:, t:])
        o = o * lax.rsqrt(jnp.mean(o * o, axis=0, keepdims=True) + EPS)
        o_ref[:, hh * DV:(hh + 1) * DV] = ((o.T * sg_ref[...]) * (1.0 - LAM_INIT)).astype(BF16)


def _attn_prompt(q, kb, vt, rel_bias, lq1, lk1, lq2, lk2, sg):
    b, s, _ = q.shape
    t = TA
    code = (_bucket_table(LANES)[None, :] * H_ATTN + np.arange(H_ATTN)[:, None]).astype(np.int32)
    code = np.where(_bucket_table(LANES)[None, :] == NUM_BUCKETS - 1, -1, code)
    smem = pl.BlockSpec(memory_space=pltpu.SMEM)
    small = lambda shape: _const_spec(shape)
    return pl.pallas_call(
        _attn_prompt_kernel,
        grid=(b, s // t),
        in_specs=[
            smem,
            small((H_ATTN, LANES)),
            small((1, HEAD_DIM)), small((1, HEAD_DIM)), small((1, HEAD_DIM)), small((1, HEAD_DIM)),
            small((1, DV)),
            pl.BlockSpec((None, t, ATT_W), lambda bi, qi: (bi, qi, 0)),
            pl.BlockSpec((None, s, ATT_W), lambda bi, qi: (bi, 0, 0)),
            pl.BlockSpec((None, ATT_W, s), lambda bi, qi: (bi, 0, 0)),
        ],
        out_specs=pl.BlockSpec((None, t, ATT_W), lambda bi, qi: (bi, qi, 0)),
        out_shape=jax.ShapeDtypeStruct((b, s, ATT_W), BF16),
        scratch_shapes=[
            pltpu.VMEM((H_ATTN, 2, t, t), F32),
            pltpu.VMEM((H_ATTN, 2 * t, DV), BF16),
            pltpu.VMEM((H_ATTN, t, 2 * t), F32),
            pltpu.VMEM((H_ATTN, t, 2 * t), F32),
            pltpu.VMEM((H_ATTN, 1, 2 * t), F32),
            pltpu.VMEM((H_ATTN, DV + 2 * SUBLANES, 2 * t), F32),
        ],
        compiler_params=pltpu.CompilerParams(
            dimension_semantics=("arbitrary", "arbitrary"), vmem_limit_bytes=VMEM_LIMIT),
        name="attn_prompt",
    )(rel_bias, jnp.asarray(code), lq1, lk1, lq2, lk2, sg, q, kb, vt)


def _page_rows(ref):
    return jnp.concatenate(
        [ref[pl.ds(hh, PAGE_SIZE, stride=H_ATTN), :] for hh in range(H_ATTN)], axis=1)


def _attn_sample_kernel(pt_ref, rb_ref, code_last_ref, code_new_ref,
                        lq1_ref, lk1_ref, lq2_ref, lk2_ref, sg_ref, q_ref, kn_ref, vn_ref, *rest):
    npg = PAGES_PER_STEP
    k_refs = rest[:npg]
    v_refs = rest[npg:2 * npg]
    o_ref, bias_ref, qbd_ref, kpad_ref, vpad_ref, m_ref, l_ref, acc_ref = rest[2 * npg:]
    del pt_ref
    g = pl.program_id(1)
    n_steps = pl.num_programs(1)
    nq = q_ref.shape[0]
    rows = 2 * H_ATTN * nq

    @pl.when((pl.program_id(0) == 0) & (g == 0))
    def _():
        bias_ref[0] = jnp.zeros((LANES, LANES), F32)
        bias_ref[1] = _bias_by_code(code_last_ref[...], rb_ref)
        code_new = code_new_ref[...]
        bias_ref[2] = jnp.where(code_new == -2, NEG_INF, _bias_by_code(code_new, rb_ref))
        kpad_ref[...] = jnp.zeros_like(kpad_ref)
        vpad_ref[...] = jnp.zeros_like(vpad_ref)
        qbd_ref[...] = jnp.zeros_like(qbd_ref)

    def update(s, v, first):
        m_cur = jnp.max(s, axis=1, keepdims=True)
        if first:
            m_new = m_cur
        else:
            m_old = m_ref[...]
            m_new = jnp.maximum(m_old, m_cur)
            alpha = jnp.exp(m_old - m_new)
        p = jnp.exp(s - m_new)
        l_cur = jnp.sum(p, axis=1, keepdims=True)
        pv = jnp.dot(p, v, preferred_element_type=F32)
        if first:
            l_ref[...] = l_cur
            acc_ref[...] = pv
        else:
            l_ref[...] = alpha * l_ref[...] + l_cur
            acc_ref[...] = alpha * acc_ref[...] + pv
        m_ref[...] = m_new

    def logits_t(kmat):
        return lax.dot_general(kmat, qbd_ref[...], (((1,), (1,)), ((), ())),
                               preferred_element_type=F32)

    @pl.when(g == 0)
    def _():
        q = q_ref[...]
        lane = lax.broadcasted_iota(jnp.int32, q.shape, 1)
        for hh in range(H_ATTN):
            for cc in range(2):
                lo = hh * DV + cc * HEAD_DIM
                r0 = (hh * 2 + cc) * nq
                qbd_ref[r0:r0 + nq, :] = jnp.where((lane >= lo) & (lane < lo + HEAD_DIM), q, 0.0)
        kpad_ref[0:nq, :] = kn_ref[...]
        vpad_ref[0:nq, :] = vn_ref[...]
        st = logits_t(kpad_ref[...]) + bias_ref[2]
        update(st.T[0:rows, :], vpad_ref[...], True)

    last = (g == n_steps - 1).astype(jnp.int32)
    st_pages = [logits_t(_page_rows(k_refs[pi])) for pi in range(npg)]
    st_pages[npg - 1] = st_pages[npg - 1] + bias_ref[last]
    st = jnp.concatenate(st_pages, axis=0)
    v = jnp.concatenate([_page_rows(v_refs[pi]) for pi in range(npg)], axis=0)
    update(st.T[0:rows, :], v, False)

    @pl.when(g == n_steps - 1)
    def _():
        lam = _lambda(lq1_ref, lk1_ref, lq2_ref, lk2_ref)
        o_all = acc_ref[...] * (1.0 / l_ref[...])
        for hh in range(H_ATTN):
            cols = slice(hh * DV, (hh + 1) * DV)
            r0 = hh * 2 * nq
            o = o_all[r0:r0 + nq, cols] - lam * o_all[r0 + nq:r0 + 2 * nq, cols]
            o = o * lax.rsqrt(jnp.mean(o * o, axis=-1, keepdims=True) + EPS)
            o_ref[:, cols] = (o * sg_ref[...]) * (1.0 - LAM_INIT)


def _attn_sample(page_table, q, kn, vn, cache_k, cache_v, rel_bias, lq1, lk1, lq2, lk2, sg):
    bs, n_pages = page_table.shape
    nq = q.shape[0] // bs
    npg = PAGES_PER_STEP
    steps = n_pages // npg
    rows = 2 * H_ATTN * nq
    bkt = _bucket_table(PAGE_SIZE + nq + 1)
    col = np.arange(LANES)
    col_h = col // (2 * nq)
    col_i = col % nq
    key = np.arange(LANES)[:, None]
    rel_last = PAGE_SIZE + col_i[None, :] - key
    code_last = bkt[rel_last] * H_ATTN + col_h[None, :]
    code_last = np.where((col[None, :] >= rows) | (bkt[rel_last] == NUM_BUCKETS - 1), -1, code_last)
    rel_new = col_i[None, :] - key
    code_new = bkt[np.maximum(rel_new, 0)] * H_ATTN + col_h[None, :]
    code_new = np.where((rel_new < 0) | (key >= nq), -2, code_new)
    code_new = np.where(col[None, :] >= rows, -1, code_new)

    smem = pl.BlockSpec(memory_space=pltpu.SMEM)
    small = lambda shape: pl.BlockSpec(shape, lambda bi, gi, pt: (0,) * len(shape))
    per_b = pl.BlockSpec((nq, ATT_W), lambda bi, gi, pt: (bi, 0))

    def page_spec(pi):
        return pl.BlockSpec((None, PAGE_SIZE * H_ATTN, DV),
                            lambda bi, gi, pt: (pt[bi, gi * npg + pi], 0, 0))

    grid_spec = pltpu.PrefetchScalarGridSpec(
        num_scalar_prefetch=1,
        grid=(bs, steps),
        in_specs=[
            smem,
            small((LANES, LANES)), small((LANES, LANES)),
            small((1, HEAD_DIM)), small((1, HEAD_DIM)), small((1, HEAD_DIM)), small((1, HEAD_DIM)),
            small((1, DV)),
            per_b, per_b, per_b,
        ] + [page_spec(pi) for pi in range(npg)] + [page_spec(pi) for pi in range(npg)],
        out_specs=per_b,
        scratch_shapes=[
            pltpu.VMEM((3, LANES, LANES), F32),
            pltpu.VMEM((LANES, ATT_W), F32),
            pltpu.VMEM((LANES, ATT_W), F32),
            pltpu.VMEM((LANES, ATT_W), F32),
            pltpu.VMEM((rows, 1), F32),
            pltpu.VMEM((rows, 1), F32),
            pltpu.VMEM((rows, ATT_W), F32),
        ],
    )
    return pl.pallas_call(
        _attn_sample_kernel,
        grid_spec=grid_spec,
        out_shape=jax.ShapeDtypeStruct((bs * nq, ATT_W), F32),
        compiler_params=pltpu.CompilerParams(
            dimension_semantics=("arbitrary", "arbitrary"), vmem_limit_bytes=VMEM_LIMIT),
        name="attn_sample",
    )(page_table, rel_bias, jnp.asarray(code_last.astype(np.int32)),
      jnp.asarray(code_new.astype(np.int32)), lq1, lk1, lq2, lk2, sg, q, kn, vn,
      *([cache_k] * npg), *([cache_v] * npg))


N_LT = 2 * FC // LANES
UP_LT = FC // LANES


def _silu_mul(z_up, z_gate):
    return (z_gate * (1.0 / (1.0 + jnp.exp(-z_gate)))) * z_up


def _chunk_col(c, lt):
    return c * FC + lt * LANES if lt < UP_LT else D_FF + c * FC + (lt - UP_LT) * LANES


def _chunk_cols_of(ref, c):
    return jnp.concatenate([ref[:, c * FC:(c + 1) * FC],
                            ref[:, D_FF + c * FC:D_FF + (c + 1) * FC]], axis=1)


def _ffn_core(x, attn, gated, wout_ref, g2_ref, wup_ref, wdn_ref, gf_ref, y_ref,
              acc_ref, h2_ref, act_ref, store_up, gate):
    mix = jnp.concatenate([attn.astype(BF16), gated], axis=1)
    x1 = x + jnp.dot(mix, wout_ref[...], preferred_element_type=F32)
    h2_ref[...] = _rms(x1, g2_ref[...]).astype(BF16)
    acc_ref[...] = x1

    def up(c):
        store_up(c, jnp.dot(h2_ref[...], _chunk_cols_of(wup_ref, c), preferred_element_type=F32))

    def down(c):
        acc_ref[...] += jnp.dot(act_ref[c % ACT_SLOTS], wdn_ref[c * FC:(c + 1) * FC, :],
                                preferred_element_type=F32)

    up(0)
    for c in range(NC):
        if c + 1 < NC:
            up(c + 1)
        if c >= 1:
            down(c - 1)
        gate(c)
    down(NC - 1)
    y_ref[...] = _rms(acc_ref[...], gf_ref[...])


def _ffn_prompt_kernel(x_ref, attn_ref, gated_ref, wout_ref, g2_ref, wup_ref, cw_ref, wdn_ref,
                       gf_ref, y_ref, fst_ref, carry_ref, acc_ref, h2_ref, zbuf_ref, act_ref):
    @pl.when(pl.program_id(1) == 0)
    def _():
        carry_ref[...] = jnp.zeros_like(carry_ref)

    tm = x_ref.shape[0]
    halo = SUBLANES
    keep = CONV_K - 1

    def store_up(c, zr):
        for lt in range(N_LT):
            zbuf_ref[c % ZR_SLOTS, lt, halo:, :] = zr[:, lt * LANES:(lt + 1) * LANES]

    def gate(c):
        slot = c % ZR_SLOTS
        zbuf_ref[slot, :, 0:halo, :] = carry_ref[c]

        def conv(lt, r):
            col = _chunk_col(c, lt)
            w = cw_ref[:, col:col + LANES]
            z0 = zbuf_ref[slot, lt, pl.ds(halo + r, GATE_ROWS), :]
            z1 = zbuf_ref[slot, lt, pl.ds(halo + r - 1, GATE_ROWS), :]
            z2 = zbuf_ref[slot, lt, pl.ds(halo + r - 2, GATE_ROWS), :]
            return _conv3(w, z0, z1, z2)

        for r in range(0, tm, GATE_ROWS):
            act = jnp.concatenate(
                [_silu_mul(conv(lt, r), conv(UP_LT + lt, r)) for lt in range(UP_LT)], axis=1)
            act_ref[c % ACT_SLOTS, r:r + GATE_ROWS, :] = act.astype(BF16)
        last = zbuf_ref[slot, :, tm:tm + halo, :]
        carry_ref[c] = last
        for lt in range(N_LT):
            col = _chunk_col(c, lt)
            fst_ref[:, col:col + LANES] = last[lt, halo - keep:, :]

    _ffn_core(x_ref[...], attn_ref[...], gated_ref[...], wout_ref, g2_ref, wup_ref,
              wdn_ref, gf_ref, y_ref, acc_ref, h2_ref, act_ref, store_up, gate)


def _ffn_sample_kernel(x_ref, attn_ref, gated_ref, wout_ref, g2_ref, wup_ref, cw_ref, wdn_ref,
                       gf_ref, pre_ref, y_ref, fst_ref, acc_ref, h2_ref, zbuf_ref, act_ref):
    tm = x_ref.shape[0]
    rows = SUBLANES
    n_seq = tm // rows
    keep = CONV_K - 1

    def store_up(c, zr):
        for lt in range(N_LT):
            zbuf_ref[c % ZR_SLOTS, lt, :, rows:, :] = (
                zr[:, lt * LANES:(lt + 1) * LANES].reshape(n_seq, rows, LANES))

    def gate(c):
        slot = c % ZR_SLOTS
        for lt in range(N_LT):
            col = _chunk_col(c, lt)
            zbuf_ref[slot, lt, :, rows - keep:rows, :] = pre_ref[:, :, col:col + LANES]

        def conv(lt):
            col = _chunk_col(c, lt)
            w = cw_ref[:, col:col + LANES]
            z0 = zbuf_ref[slot, lt, :, rows:, :]
            z1 = zbuf_ref[slot, lt, :, rows - 1:2 * rows - 1, :]
            z2 = zbuf_ref[slot, lt, :, rows - 2:2 * rows - 2, :]
            return _conv3(w, z0, z1, z2)

        act = jnp.concatenate(
            [_silu_mul(conv(lt), conv(UP_LT + lt)).reshape(tm, LANES) for lt in range(UP_LT)],
            axis=1)
        act_ref[c % ACT_SLOTS] = act.astype(BF16)
        for lt in range(N_LT):
            col = _chunk_col(c, lt)
            fst_ref[:, :, col:col + LANES] = zbuf_ref[slot, lt, :, 2 * rows - keep:, :]

    _ffn_core(x_ref[...], attn_ref[...], gated_ref[...], wout_ref, g2_ref, wup_ref,
              wdn_ref, gf_ref, y_ref, acc_ref, h2_ref, act_ref, store_up, gate)


def _ffn_weight_specs():
    return [
        _const_spec((D_MODEL, D_MODEL)),
        _const_spec((1, D_MODEL)),
        _const_spec((D_MODEL, 2 * D_FF)),
        _const_spec((CONV_K, 2 * D_FF)),
        _const_spec((D_FF, D_MODEL)),
        _const_spec((1, D_MODEL)),
    ]


def _ffn_prompt(x, attn, gated, wout, g2, wup, cw, wdn, gf):
    b, s, d = x.shape
    tok = lambda bi, ti: (bi, ti, 0)
    return pl.pallas_call(
        _ffn_prompt_kernel,
        grid=(b, s // TM),
        in_specs=[
            pl.BlockSpec((None, TM, d), tok),
            pl.BlockSpec((None, TM, ATT_W), tok),
            pl.BlockSpec((None, TM, CONV_W), tok),
        ] + _ffn_weight_specs(),
        out_specs=(
            pl.BlockSpec((None, TM, d), tok),
            pl.BlockSpec((None, CONV_K - 1, 2 * D_FF), lambda bi, ti: (bi, 0, 0)),
        ),
        out_shape=(
            jax.ShapeDtypeStruct((b, s, d), F32),
            jax.ShapeDtypeStruct((b, CONV_K - 1, 2 * D_FF), F32),
        ),
        scratch_shapes=[
            pltpu.VMEM((NC, N_LT, SUBLANES, LANES), F32),
            pltpu.VMEM((TM, d), F32),
            pltpu.VMEM((TM, d), BF16),
            pltpu.VMEM((ZR_SLOTS, N_LT, SUBLANES + TM, LANES), F32),
            pltpu.VMEM((ACT_SLOTS, TM, FC), BF16),
        ],
        compiler_params=pltpu.CompilerParams(
            dimension_semantics=("arbitrary", "arbitrary"), vmem_limit_bytes=VMEM_LIMIT),
        name="ffn_prompt",
    )(x, attn, gated, wout, g2, wup, cw, wdn, gf)


def _ffn_sample(x, attn, gated, wout, g2, wup, cw, wdn, gf, state):
    n, d = x.shape
    n_seq = TMS // SUBLANES
    tok = lambda ti: (ti, 0)
    seq = lambda ti: (ti, 0, 0)
    return pl.pallas_call(
        _ffn_sample_kernel,
        grid=(n // TMS,),
        in_specs=[
            pl.BlockSpec((TMS, d), tok),
            pl.BlockSpec((TMS, ATT_W), tok),
            pl.BlockSpec((TMS, CONV_W), tok),
        ] + _ffn_weight_specs() + [
            pl.BlockSpec((n_seq, CONV_K - 1, 2 * D_FF), seq),
        ],
        out_specs=(
            pl.BlockSpec((TMS, d), tok),
            pl.BlockSpec((n_seq, CONV_K - 1, 2 * D_FF), seq),
        ),
        out_shape=(
            jax.ShapeDtypeStruct((n, d), F32),
            jax.ShapeDtypeStruct(state.shape, F32),
        ),
        scratch_shapes=[
            pltpu.VMEM((TMS, d), F32),
            pltpu.VMEM((TMS, d), BF16),
            pltpu.VMEM((ZR_SLOTS, N_LT, n_seq, 2 * SUBLANES, LANES), F32),
            pltpu.VMEM((ACT_SLOTS, TMS, FC), BF16),
        ],
        compiler_params=pltpu.CompilerParams(
            dimension_semantics=("arbitrary",), vmem_limit_bytes=VMEM_LIMIT),
        name="ffn_sample",
    )(x, attn, gated, wout, g2, wup, cw, wdn, gf, state)


def _group_prefix(state):
    bsz, k, c = state.shape
    pad = jnp.zeros((bsz, SUBLANES - k, c), state.dtype)
    return jnp.concatenate([state, pad], axis=1).reshape(bsz * SUBLANES, c)


def kernel(x_prompt, x_sample, cache_k, cache_v, state_conv, state_ffn, page_table, norm_mix_g, w_in, conv_w, lambda_q1, lambda_k1, lambda_q2, lambda_k2, subln_g, rel_bias, w_out, norm_ffn_g, w_up, ffn_conv_w, w_down, norm_final_g):
    depth = w_in.shape[0]
    assert depth == 1
    bp, seq, d = x_prompt.shape
    bs, dec, _ = x_sample.shape
    assert dec == SUBLANES and seq % TM == 0 and seq % TA == 0 and (bs * dec) % TMS == 0
    n_pool = cache_k.shape[1]
    layer = 0

    g1 = norm_mix_g[layer].reshape(1, d)
    g2 = norm_ffn_g[layer].reshape(1, d)
    gf = norm_final_g.reshape(1, d)
    w_in_b = w_in[layer].astype(BF16)
    w_out_b = w_out[layer].astype(BF16)
    w_up_b = w_up[layer].astype(BF16)
    w_dn_b = w_down[layer].astype(BF16)
    cw = conv_w[layer]
    fcw = ffn_conv_w[layer]
    lq1 = lambda_q1[layer].reshape(1, HEAD_DIM)
    lk1 = lambda_k1[layer].reshape(1, HEAD_DIM)
    lq2 = lambda_q2[layer].reshape(1, HEAD_DIM)
    lk2 = lambda_k2[layer].reshape(1, HEAD_DIM)
    sg = subln_g[layer].reshape(1, DV)

    q_p, k4_p, v4_p, kb_p, vt_p, gated_p, cst_p = _inproj_prompt(x_prompt, g1, w_in_b, cw)
    attn_p = _attn_prompt(q_p, kb_p, vt_p, rel_bias, lq1, lk1, lq2, lk2, sg)
    y_p, fst_p = _ffn_prompt(x_prompt, attn_p, gated_p, w_out_b, g2, w_up_b, fcw, w_dn_b, gf)

    xs = x_sample.reshape(bs * dec, d)
    q_s, k_s, v_s, gated_s, cu_s = _inproj_sample(xs, g1, w_in_b, cw, _group_prefix(state_conv[layer]))
    ck = cache_k.reshape(depth * n_pool, PAGE_SIZE * H_ATTN, DV)
    cv = cache_v.reshape(depth * n_pool, PAGE_SIZE * H_ATTN, DV)
    attn_s = _attn_sample(page_table, q_s, k_s, v_s, ck, cv, rel_bias, lq1, lk1, lq2, lk2, sg)
    y_s, fst_s = _ffn_sample(xs, attn_s, gated_s, w_out_b, g2, w_up_b, fcw, w_dn_b, gf,
                             state_ffn[layer])

    keep = CONV_K - 1
    y_prompt = y_p
    y_sample = y_s.reshape(bs, dec, d)
    new_k_prompt = k4_p.reshape(1, bp, seq, H_ATTN, DV)
    new_v_prompt = v4_p.reshape(1, bp, seq, H_ATTN, DV)
    new_conv_prompt = cst_p[None]
    new_ffn_prompt = fst_p[None]
    new_k_sample = k_s.reshape(1, bs, dec, H_ATTN, DV)
    new_v_sample = v_s.reshape(1, bs, dec, H_ATTN, DV)
    new_conv_sample = cu_s.reshape(bs, dec, CONV_W)[:, dec - keep:, :][None]
    new_ffn_sample = fst_s[None]
    return (y_prompt, y_sample, new_k_prompt, new_v_prompt, new_conv_prompt, new_ffn_prompt,
            new_k_sample, new_v_sample, new_conv_sample, new_ffn_sample)
```

```python
import functools
import math

import jax
import jax.numpy as jnp
import numpy as np
from jax import lax
from jax.experimental import pallas as pl
from jax.experimental.pallas import tpu as pltpu

D_MODEL = 1024
H_ATTN = 4
HEAD_DIM = 64
DV = 2 * HEAD_DIM
ATT_W = H_ATTN * DV
CONV_W = D_MODEL - ATT_W
CONV_K = 3
D_FF = 2816
NUM_BUCKETS = 32
MAX_EXACT = NUM_BUCKETS // 2
MAX_DISTANCE = 128
PAGE_SIZE = 128
EPS = 1e-6
SCALE = HEAD_DIM ** -0.5
LOG2E = math.log2(math.e)
NEG_INF = -1e30
LAM_INIT = 0.8 - 0.6 * math.exp(-0.3 * 0)

LANES = 128
SUBLANES = 8
VMEM_LIMIT = 56 * 1024 * 1024

TM = 512
TMS = 256
FC = 256
NC = D_FF // FC
GATE_ROWS = 64
ZR_SLOTS = 3
ACT_SLOTS = 2
TA = 256
PAGES_PER_STEP = 8

F32 = jnp.float32
BF16 = jnp.bfloat16


def _bucket_table(n_max):
    n = np.arange(n_max)
    nf = np.maximum(n, MAX_EXACT).astype(np.float64)
    large = MAX_EXACT + (np.log(nf / MAX_EXACT) / math.log(MAX_DISTANCE / MAX_EXACT)
                         * (NUM_BUCKETS - MAX_EXACT)).astype(np.int64)
    large = np.minimum(large, NUM_BUCKETS - 1)
    return np.where(n < MAX_EXACT, n, large).astype(np.int32)


def _rms(x, g):
    return x * lax.rsqrt(jnp.mean(x * x, axis=-1, keepdims=True) + EPS) * g


def _shift_rows_seq(z, carry):
    row = lax.broadcasted_iota(jnp.int32, z.shape, 0)
    c6 = carry[6:7, :]
    c7 = carry[7:8, :]
    z1 = jnp.where(row == 0, c7, pltpu.roll(z, 1, 0))
    z2 = jnp.where(row == 0, c6, jnp.where(row == 1, c7, pltpu.roll(z, 2, 0)))
    return z1, z2


def _shift_rows_groups(z, prefix):
    row = lax.broadcasted_iota(jnp.int32, z.shape, 0) % SUBLANES
    up_one = pltpu.roll(prefix, prefix.shape[0] - 1, 0)
    z1 = jnp.where(row >= 1, pltpu.roll(z, 1, 0), up_one)
    z2 = jnp.where(row >= 2, pltpu.roll(z, 2, 0), prefix)
    return z1, z2


def _conv3(w, z, z1, z2):
    return w[0:1, :] * z2 + w[1:2, :] * z1 + w[2:3, :] * z


def _project(x, g_ref, w_ref):
    h = _rms(x, g_ref[...]).astype(BF16)
    return jnp.dot(h, w_ref[...], preferred_element_type=F32)


def _store_heads(dst_ref, val):
    rows = val.shape[0]
    for hh in range(H_ATTN):
        dst_ref[pl.ds(hh, rows, stride=H_ATTN), :] = val[:, hh * DV:(hh + 1) * DV]


def _inproj_prompt_kernel(x_ref, g_ref, w_ref, cw_ref, q_ref, k4_ref, v4_ref, kb_ref, vt_ref,
                          gated_ref, cst_ref, carry_ref):
    @pl.when(pl.program_id(1) == 0)
    def _():
        carry_ref[...] = jnp.zeros_like(carry_ref)

    proj = _project(x_ref[...], g_ref, w_ref)
    q = proj[:, 0:ATT_W]
    k = proj[:, ATT_W:2 * ATT_W]
    v = proj[:, 2 * ATT_W:3 * ATT_W]
    gate_b = proj[:, 3 * ATT_W:3 * ATT_W + CONV_W]
    gate_c = proj[:, 3 * ATT_W + CONV_W:3 * ATT_W + 2 * CONV_W]
    u = proj[:, 3 * ATT_W + 2 * CONV_W:]
    q_ref[...] = (q * (SCALE * LOG2E)).astype(BF16)
    kb_ref[...] = k.astype(BF16)
    vt_ref[...] = v.T.astype(BF16)
    _store_heads(k4_ref, k)
    _store_heads(v4_ref, v)
    cu = gate_c * u
    z1, z2 = _shift_rows_seq(cu, carry_ref[...])
    gated_ref[...] = (gate_b * _conv3(cw_ref[...], cu, z1, z2)).astype(BF16)
    rows = cu.shape[0]
    carry_ref[...] = cu[rows - SUBLANES:, :]
    cst_ref[...] = cu[rows - (CONV_K - 1):, :]


def _inproj_sample_kernel(x_ref, g_ref, w_ref, cw_ref, pre_ref, q_ref, k_ref, v_ref, gated_ref,
                          cu_ref):
    proj = _project(x_ref[...], g_ref, w_ref)
    q_ref[...] = proj[:, 0:ATT_W] * SCALE
    k_ref[...] = proj[:, ATT_W:2 * ATT_W]
    v_ref[...] = proj[:, 2 * ATT_W:3 * ATT_W]
    gate_b = proj[:, 3 * ATT_W:3 * ATT_W + CONV_W]
    gate_c = proj[:, 3 * ATT_W + CONV_W:3 * ATT_W + 2 * CONV_W]
    u = proj[:, 3 * ATT_W + 2 * CONV_W:]
    cu = gate_c * u
    z1, z2 = _shift_rows_groups(cu, pre_ref[...])
    gated_ref[...] = (gate_b * _conv3(cw_ref[...], cu, z1, z2)).astype(BF16)
    cu_ref[...] = cu


def _const_spec(shape):
    nd = len(shape)
    return pl.BlockSpec(shape, lambda *_: (0,) * nd, pipeline_mode=pl.Buffered(1))


def _inproj_prompt(x, g, w, cw):
    b, s, d = x.shape
    nt = s // TM
    e = w.shape[1]
    out_shape = (
        jax.ShapeDtypeStruct((b, s, ATT_W), BF16),
        jax.ShapeDtypeStruct((b, s * H_ATTN, DV), F32),
        jax.ShapeDtypeStruct((b, s * H_ATTN, DV), F32),
        jax.ShapeDtypeStruct((b, s, ATT_W), BF16),
        jax.ShapeDtypeStruct((b, ATT_W, s), BF16),
        jax.ShapeDtypeStruct((b, s, CONV_W), BF16),
        jax.ShapeDtypeStruct((b, CONV_K - 1, CONV_W), F32),
    )
    tok = lambda bi, ti: (bi, ti, 0)
    return pl.pallas_call(
        _inproj_prompt_kernel,
        grid=(b, nt),
        in_specs=[
            pl.BlockSpec((None, TM, d), tok),
            _const_spec((1, d)),
            _const_spec((d, e)),
            _const_spec((CONV_K, CONV_W)),
        ],
        out_specs=(
            pl.BlockSpec((None, TM, ATT_W), tok),
            pl.BlockSpec((None, TM * H_ATTN, DV), tok),
            pl.BlockSpec((None, TM * H_ATTN, DV), tok),
            pl.BlockSpec((None, TM, ATT_W), tok),
            pl.BlockSpec((None, ATT_W, TM), lambda bi, ti: (bi, 0, ti)),
            pl.BlockSpec((None, TM, CONV_W), tok),
            pl.BlockSpec((None, CONV_K - 1, CONV_W), lambda bi, ti: (bi, 0, 0)),
        ),
        out_shape=out_shape,
        scratch_shapes=[pltpu.VMEM((SUBLANES, CONV_W), F32)],
        compiler_params=pltpu.CompilerParams(
            dimension_semantics=("arbitrary", "arbitrary"), vmem_limit_bytes=VMEM_LIMIT),
        name="inproj_prompt",
    )(x, g, w, cw)


def _inproj_sample(x, g, w, cw, prefix):
    n, d = x.shape
    e = w.shape[1]
    nt = n // TMS
    tok = lambda ti: (ti, 0)
    out_shape = (
        jax.ShapeDtypeStruct((n, ATT_W), F32),
        jax.ShapeDtypeStruct((n, ATT_W), F32),
        jax.ShapeDtypeStruct((n, ATT_W), F32),
        jax.ShapeDtypeStruct((n, CONV_W), BF16),
        jax.ShapeDtypeStruct((n, CONV_W), F32),
    )
    return pl.pallas_call(
        _inproj_sample_kernel,
        grid=(nt,),
        in_specs=[
            pl.BlockSpec((TMS, d), tok),
            _const_spec((1, d)),
            _const_spec((d, e)),
            _const_spec((CONV_K, CONV_W)),
            pl.BlockSpec((TMS, CONV_W), tok),
        ],
        out_specs=(
            pl.BlockSpec((TMS, ATT_W), tok),
            pl.BlockSpec((TMS, ATT_W), tok),
            pl.BlockSpec((TMS, ATT_W), tok),
            pl.BlockSpec((TMS, CONV_W), tok),
            pl.BlockSpec((TMS, CONV_W), tok),
        ),
        out_shape=out_shape,
        compiler_params=pltpu.CompilerParams(
            dimension_semantics=("arbitrary",), vmem_limit_bytes=VMEM_LIMIT),
        name="inproj_sample",
    )(x, g, w, cw, prefix)


def _lambda(lq1_ref, lk1_ref, lq2_ref, lk2_ref):
    s1 = jnp.sum(lq1_ref[...] * lk1_ref[...], axis=-1, keepdims=True)
    s2 = jnp.sum(lq2_ref[...] * lk2_ref[...], axis=-1, keepdims=True)
    return jnp.exp(s1) - jnp.exp(s2) + LAM_INIT


def _bias_by_code(code, rb_ref, scale=1.0):
    val = jnp.zeros(code.shape, F32)
    for bk in range(NUM_BUCKETS - 1):
        for hh in range(H_ATTN):
            val = jnp.where(code == bk * H_ATTN + hh,
                            (rb_ref[bk, hh] - rb_ref[NUM_BUCKETS - 1, hh]) * scale, val)
    return val


def _attn_prompt_kernel(rb_ref, code_ref, lq1_ref, lk1_ref, lq2_ref, lk2_ref, sg_ref,
                        q_ref, k_ref, vt_ref, o_ref, bias_ref, q2_ref, sa_ref, sb_ref, m_ref, acc_ref):
    t = TA
    nb = t // LANES
    i = pl.program_id(1)

    @pl.when((pl.program_id(0) == 0) & (i == 0))
    def _():
        r = lax.broadcasted_iota(jnp.int32, (LANES, LANES), 0)
        c = lax.broadcasted_iota(jnp.int32, (LANES, LANES), 1)
        base = _bias_by_code(code_ref[...], rb_ref, LOG2E)
        for hh in range(H_ATTN):
            rows = jnp.broadcast_to(base[hh:hh + 1, :], (LANES, LANES))
            circ = pltpu.roll(rows, 0, 1, stride=1, stride_axis=0)
            near = jnp.where(c >= r, circ, NEG_INF)
            wrap = jnp.where(c < r, circ, 0.0)
            zero = jnp.zeros((LANES, LANES), F32)
            for kb in range(nb):
                for qb in range(nb):
                    if qb < kb:
                        blk = jnp.full((LANES, LANES), NEG_INF, F32)
                    elif qb == kb:
                        blk = near
                    elif qb == kb + 1:
                        blk = wrap
                    else:
                        blk = zero
                    bias_ref[hh, 0, kb * LANES:(kb + 1) * LANES, qb * LANES:(qb + 1) * LANES] = blk
                    sub = wrap if (kb == nb - 1 and qb == 0) else zero
                    bias_ref[hh, 1, kb * LANES:(kb + 1) * LANES, qb * LANES:(qb + 1) * LANES] = sub

    lam = _lambda(lq1_ref, lk1_ref, lq2_ref, lk2_ref)
    lane = lax.broadcasted_iota(jnp.int32, (t, DV), 1)
    ones = jnp.ones((2 * SUBLANES, t), BF16)

    for hh in range(H_ATTN):
        qh = q_ref[:, hh * DV:(hh + 1) * DV]
        q2_ref[hh] = jnp.concatenate([jnp.where(lane < HEAD_DIM, qh, jnp.zeros_like(qh)),
                                      jnp.where(lane >= HEAD_DIM, qh, jnp.zeros_like(qh))], axis=0)
    m_ref[...] = jnp.full(m_ref.shape, NEG_INF, F32)
    acc_ref[...] = jnp.zeros_like(acc_ref)

    def logits_t(hh, j):
        start = pl.multiple_of(j * t, t)
        kj = k_ref[pl.ds(start, t), hh * DV:(hh + 1) * DV]
        return lax.dot_general(kj, q2_ref[hh], (((1,), (1,)), ((), ())),
                               preferred_element_type=F32)

    def tile(j, cur_ref, nxt_ref, kind, prefetch):
        start = pl.multiple_of(j * t, t)
        for hh in range(H_ATTN):
            cols = slice(hh * DV, (hh + 1) * DV)
            if prefetch:
                nxt_ref[hh] = logits_t(hh, j + 1)
            st = cur_ref[hh]
            if kind is not None:
                bias = bias_ref[hh, kind]
                st = jnp.concatenate([st[:, :t] + bias, st[:, t:] + bias], axis=1)
            m_old = m_ref[hh]
            m_new = jnp.maximum(m_old, jnp.max(st, axis=0, keepdims=True))
            alpha = jnp.exp2(m_old - m_new)
            p = jnp.exp2(st - m_new).astype(BF16)
            vt = jnp.concatenate([vt_ref[cols, pl.ds(start, t)], ones], axis=0)
            pv = jnp.dot(vt, p, preferred_element_type=F32)
            acc_ref[hh] = alpha * acc_ref[hh] + pv
            m_ref[hh] = m_new

    for hh in range(H_ATTN):
        sa_ref[hh] = logits_t(hh, 0)

    n_pairs = jnp.maximum(i - 1, 0) // 2

    def far_pair(jj, carry):
        tile(2 * jj, sa_ref, sb_ref, None, True)
        tile(2 * jj + 1, sb_ref, sa_ref, None, True)
        return carry

    lax.fori_loop(0, n_pairs, far_pair, 0)

    @pl.when(i == 0)
    def _():
        tile(i, sa_ref, sb_ref, 0, False)

    @pl.when(i % 2 == 1)
    def _():
        tile(i - 1, sa_ref, sb_ref, 1, True)
        tile(i, sb_ref, sa_ref, 0, False)

    @pl.when((i % 2 == 0) & (i >= 2))
    def _():
        tile(i - 2, sa_ref, sb_ref, None, True)
        tile(i - 1, sb_ref, sa_ref, 1, True)
        tile(i, sa_ref, sb_ref, 0, False)

    for hh in range(H_ATTN):
        acc = acc_ref[hh]
        inv_l = 1.0 / acc[DV:DV + 1, :]
        o = acc[0:DV, 0:t] * inv_l[:, 0:t] - lam * (acc[0:DV, t:] * inv_l[:, t:])
        o = o * lax.rsqrt(jnp.mean(o * o, axis=0, keepdims=True) + EPS)
        o_ref[:, hh * DV:(hh + 1) * DV] = ((o.T * sg_ref[...]) * (1.0 - LAM_INIT)).astype(BF16)


def _attn_prompt(q, kb, vt, rel_bias, lq1, lk1, lq2, lk2, sg):
    b, s, _ = q.shape
    t = TA
    code = (_bucket_table(LANES)[None, :] * H_ATTN + np.arange(H_ATTN)[:, None]).astype(np.int32)
    code = np.where(_bucket_table(LANES)[None, :] == NUM_BUCKETS - 1, -1, code)
    smem = pl.BlockSpec(memory_space=pltpu.SMEM)
    small = lambda shape: _const_spec(shape)
    return pl.pallas_call(
        _attn_prompt_kernel,
        grid=(b, s // t),
        in_specs=[
            smem,
            small((H_ATTN, LANES)),
            small((1, HEAD_DIM)), small((1, HEAD_DIM)), small((1, HEAD_DIM)), small((1, HEAD_DIM)),
            small((1, DV)),
            pl.BlockSpec((None, t, ATT_W), lambda bi, qi: (bi, qi, 0)),
            pl.BlockSpec((None, s, ATT_W), lambda bi, qi: (bi, 0, 0)),
            pl.BlockSpec((None, ATT_W, s), lambda bi, qi: (bi, 0, 0)),
        ],
        out_specs=pl.BlockSpec((None, t, ATT_W), lambda bi, qi: (bi, qi, 0)),
        out_shape=jax.ShapeDtypeStruct((b, s, ATT_W), BF16),
        scratch_shapes=[
            pltpu.VMEM((H_ATTN, 2, t, t), F32),
            pltpu.VMEM((H_ATTN, 2 * t, DV), BF16),
            pltpu.VMEM((H_ATTN, t, 2 * t), F32),
            pltpu.VMEM((H_ATTN, t, 2 * t), F32),
            pltpu.VMEM((H_ATTN, 1, 2 * t), F32),
            pltpu.VMEM((H_ATTN, DV + 2 * SUBLANES, 2 * t), F32),
        ],
        compiler_params=pltpu.CompilerParams(
            dimension_semantics=("arbitrary", "arbitrary"), vmem_limit_bytes=VMEM_LIMIT),
        name="attn_prompt",
    )(rel_bias, jnp.asarray(code), lq1, lk1, lq2, lk2, sg, q, kb, vt)


def _page_rows(ref):
    return jnp.concatenate(
        [ref[pl.ds(hh, PAGE_SIZE, stride=H_ATTN), :] for hh in range(H_ATTN)], axis=1)


def _attn_sample_kernel(pt_ref, rb_ref, code_last_ref, code_new_ref,
                        lq1_ref, lk1_ref, lq2_ref, lk2_ref, sg_ref, q_ref, kn_ref, vn_ref,
                        ck_hbm, cv_hbm, o_ref, bias_ref, qbd_ref, kpad_ref, vpad_ref,
                        m_ref, l_ref, acc_ref, kbuf_ref, vbuf_ref, sem_ref):
    npg = PAGES_PER_STEP
    n_phase = pt_ref.shape[1] // npg
    b = pl.program_id(0)
    nq = q_ref.shape[0]
    rows = 2 * H_ATTN * nq

    def page_copies(seq, phase, slot):
        copies = []
        for pi in range(npg):
            page = pt_ref[seq, phase * npg + pi]
            copies.append(pltpu.make_async_copy(
                ck_hbm.at[page], kbuf_ref.at[slot, pi], sem_ref.at[slot, 0]))
            copies.append(pltpu.make_async_copy(
                cv_hbm.at[page], vbuf_ref.at[slot, pi], sem_ref.at[slot, 1]))
        return copies

    def start_phase(seq, phase, slot):
        for n, cp in enumerate(page_copies(seq, phase, slot)):
            cp.start(priority=n % 2)

    def wait_phase(seq, phase, slot):
        for cp in page_copies(seq, phase, slot):
            cp.wait()

    @pl.when(b == 0)
    def _():
        start_phase(0, 0, 0)
        bias_ref[0] = _bias_by_code(code_last_ref[...], rb_ref)
        code_new = code_new_ref[...]
        bias_ref[1] = jnp.where(code_new == -2, NEG_INF, _bias_by_code(code_new, rb_ref))
        kpad_ref[...] = jnp.zeros_like(kpad_ref)
        vpad_ref[...] = jnp.zeros_like(vpad_ref)
        qbd_ref[...] = jnp.zeros_like(qbd_ref)

    def update(s, v, first):
        m_cur = jnp.max(s, axis=1, keepdims=True)
        if first:
            m_new = m_cur
        else:
            m_old = m_ref[...]
            m_new = jnp.maximum(m_old, m_cur)
            alpha = jnp.exp(m_old - m_new)
        p = jnp.exp(s - m_new)
        l_cur = jnp.sum(p, axis=1, keepdims=True)
        pv = jnp.dot(p, v, preferred_element_type=F32)
        if first:
            l_ref[...] = l_cur
            acc_ref[...] = pv
        else:
            l_ref[...] = alpha * l_ref[...] + l_cur
            acc_ref[...] = alpha * acc_ref[...] + pv
        m_ref[...] = m_new

    def logits_t(kmat):
        return lax.dot_general(kmat, qbd_ref[...], (((1,), (1,)), ((), ())),
                               preferred_element_type=F32)

    q = q_ref[...]
    lane = lax.broadcasted_iota(jnp.int32, q.shape, 1)
    for hh in range(H_ATTN):
        for cc in range(2):
            lo = hh * DV + cc * HEAD_DIM
            r0 = (hh * 2 + cc) * nq
            qbd_ref[r0:r0 + nq, :] = jnp.where((lane >= lo) & (lane < lo + HEAD_DIM), q, 0.0)
    kpad_ref[0:nq, :] = kn_ref[...]
    vpad_ref[0:nq, :] = vn_ref[...]
    st = logits_t(kpad_ref[...]) + bias_ref[1]
    update(st.T[0:rows, :], vpad_ref[...], True)

    for phase in range(n_phase):
        slot = phase % 2
        wait_phase(b, phase, slot)
        if phase + 1 < n_phase:
            start_phase(b, phase + 1, 1 - slot)
        else:
            @pl.when(b + 1 < pl.num_programs(0))
            def _():
                start_phase(b + 1, 0, 1 - slot)
        st_pages = [logits_t(_page_rows(kbuf_ref.at[slot, pi])) for pi in range(npg)]
        if phase == n_phase - 1:
            st_pages[npg - 1] = st_pages[npg - 1] + bias_ref[0]
        st = jnp.concatenate(st_pages, axis=0)
        v = jnp.concatenate([_page_rows(vbuf_ref.at[slot, pi]) for pi in range(npg)], axis=0)
        update(st.T[0:rows, :], v, False)

    lam = _lambda(lq1_ref, lk1_ref, lq2_ref, lk2_ref)
    o_all = acc_ref[...] * (1.0 / l_ref[...])
    for hh in range(H_ATTN):
        cols = slice(hh * DV, (hh + 1) * DV)
        r0 = hh * 2 * nq
        o = o_all[r0:r0 + nq, cols] - lam * o_all[r0 + nq:r0 + 2 * nq, cols]
        o = o * lax.rsqrt(jnp.mean(o * o, axis=-1, keepdims=True) + EPS)
        o_ref[:, cols] = (o * sg_ref[...]) * (1.0 - LAM_INIT)


def _attn_sample(page_table, q, kn, vn, cache_k, cache_v, rel_bias, lq1, lk1, lq2, lk2, sg):
    bs, n_pages = page_table.shape
    nq = q.shape[0] // bs
    npg = PAGES_PER_STEP
    assert n_pages % (2 * npg) == 0
    rows = 2 * H_ATTN * nq
    bkt = _bucket_table(PAGE_SIZE + nq + 1)
    col = np.arange(LANES)
    col_h = col // (2 * nq)
    col_i = col % nq
    key = np.arange(LANES)[:, None]
    rel_last = PAGE_SIZE + col_i[None, :] - key
    code_last = bkt[rel_last] * H_ATTN + col_h[None, :]
    code_last = np.where((col[None, :] >= rows) | (bkt[rel_last] == NUM_BUCKETS - 1), -1, code_last)
    rel_new = col_i[None, :] - key
    code_new = bkt[np.maximum(rel_new, 0)] * H_ATTN + col_h[None, :]
    code_new = np.where((rel_new < 0) | (key >= nq), -2, code_new)
    code_new = np.where(col[None, :] >= rows, -1, code_new)

    smem = pl.BlockSpec(memory_space=pltpu.SMEM)
    hbm = pl.BlockSpec(memory_space=pl.ANY)
    small = lambda shape: pl.BlockSpec(shape, lambda bi, pt: (0,) * len(shape))
    per_b = pl.BlockSpec((nq, ATT_W), lambda bi, pt: (bi, 0))
    page_buf = pltpu.VMEM((2, npg, PAGE_SIZE * H_ATTN, DV), F32)

    grid_spec = pltpu.PrefetchScalarGridSpec(
        num_scalar_prefetch=1,
        grid=(bs,),
        in_specs=[
            smem,
            small((LANES, LANES)), small((LANES, LANES)),
            small((1, HEAD_DIM)), small((1, HEAD_DIM)), small((1, HEAD_DIM)), small((1, HEAD_DIM)),
            small((1, DV)),
            per_b, per_b, per_b,
            hbm, hbm,
        ],
        out_specs=per_b,
        scratch_shapes=[
            pltpu.VMEM((2, LANES, LANES), F32),
            pltpu.VMEM((LANES, ATT_W), F32),
            pltpu.VMEM((LANES, ATT_W), F32),
            pltpu.VMEM((LANES, ATT_W), F32),
            pltpu.VMEM((rows, 1), F32),
            pltpu.VMEM((rows, 1), F32),
            pltpu.VMEM((rows, ATT_W), F32),
            page_buf,
            page_buf,
            pltpu.SemaphoreType.DMA((2, 2)),
        ],
    )
    return pl.pallas_call(
        _attn_sample_kernel,
        grid_spec=grid_spec,
        out_shape=jax.ShapeDtypeStruct((bs * nq, ATT_W), F32),
        compiler_params=pltpu.CompilerParams(
            dimension_semantics=("arbitrary",), vmem_limit_bytes=VMEM_LIMIT),
        name="attn_sample",
    )(page_table, rel_bias, jnp.asarray(code_last.astype(np.int32)),
      jnp.asarray(code_new.astype(np.int32)), lq1, lk1, lq2, lk2, sg, q, kn, vn,
      cache_k, cache_v)


N_LT = 2 * FC // LANES
UP_LT = FC // LANES


def _silu_mul(z_up, z_gate):
    return (z_gate * (1.0 / (1.0 + jnp.exp(-z_gate)))) * z_up


def _chunk_col(c, lt):
    return c * FC + lt * LANES if lt < UP_LT else D_FF + c * FC + (lt - UP_LT) * LANES


def _chunk_cols_of(ref, c):
    return jnp.concatenate([ref[:, c * FC:(c + 1) * FC],
                            ref[:, D_FF + c * FC:D_FF + (c + 1) * FC]], axis=1)


def _ffn_core(x, attn, gated, wout_ref, g2_ref, wup_ref, wdn_ref, gf_ref, y_ref,
              acc_ref, h2_ref, act_ref, store_up, gate):
    mix = jnp.concatenate([attn.astype(BF16), gated], axis=1)
    x1 = x + jnp.dot(mix, wout_ref[...], preferred_element_type=F32)
    h2_ref[...] = _rms(x1, g2_ref[...]).astype(BF16)
    acc_ref[...] = x1

    def up(c):
        store_up(c, jnp.dot(h2_ref[...], _chunk_cols_of(wup_ref, c), preferred_element_type=F32))

    def down(c):
        acc_ref[...] += jnp.dot(act_ref[c % ACT_SLOTS], wdn_ref[c * FC:(c + 1) * FC, :],
                                preferred_element_type=F32)

    up(0)
    for c in range(NC):
        if c + 1 < NC:
            up(c + 1)
        if c >= 1:
            down(c - 1)
        gate(c)
    down(NC - 1)
    y_ref[...] = _rms(acc_ref[...], gf_ref[...])


def _ffn_prompt_kernel(x_ref, attn_ref, gated_ref, wout_ref, g2_ref, wup_ref, cw_ref, wdn_ref,
                       gf_ref, y_ref, fst_ref, carry_ref, acc_ref, h2_ref, zbuf_ref, act_ref):
    @pl.when(pl.program_id(1) == 0)
    def _():
        carry_ref[...] = jnp.zeros_like(carry_ref)

    tm = x_ref.shape[0]
    halo = SUBLANES
    keep = CONV_K - 1

    def store_up(c, zr):
        for lt in range(N_LT):
            zbuf_ref[c % ZR_SLOTS, lt, halo:, :] = zr[:, lt * LANES:(lt + 1) * LANES]

    def gate(c):
        slot = c % ZR_SLOTS
        zbuf_ref[slot, :, 0:halo, :] = carry_ref[c]

        def conv(lt, r):
            col = _chunk_col(c, lt)
            w = cw_ref[:, col:col + LANES]
            z0 = zbuf_ref[slot, lt, pl.ds(halo + r, GATE_ROWS), :]
            z1 = zbuf_ref[slot, lt, pl.ds(halo + r - 1, GATE_ROWS), :]
            z2 = zbuf_ref[slot, lt, pl.ds(halo + r - 2, GATE_ROWS), :]
            return _conv3(w, z0, z1, z2)

        for r in range(0, tm, GATE_ROWS):
            act = jnp.concatenate(
                [_silu_mul(conv(lt, r), conv(UP_LT + lt, r)) for lt in range(UP_LT)], axis=1)
            act_ref[c % ACT_SLOTS, r:r + GATE_ROWS, :] = act.astype(BF16)
        last = zbuf_ref[slot, :, tm:tm + halo, :]
        carry_ref[c] = last
        for lt in range(N_LT):
            col = _chunk_col(c, lt)
            fst_ref[:, col:col + LANES] = last[lt, halo - keep:, :]

    _ffn_core(x_ref[...], attn_ref[...], gated_ref[...], wout_ref, g2_ref, wup_ref,
              wdn_ref, gf_ref, y_ref, acc_ref, h2_ref, act_ref, store_up, gate)


def _ffn_sample_kernel(x_ref, attn_ref, gated_ref, wout_ref, g2_ref, wup_ref, cw_ref, wdn_ref,
                       gf_ref, pre_ref, y_ref, fst_ref, acc_ref, h2_ref, zbuf_ref, act_ref):
    tm = x_ref.shape[0]
    rows = SUBLANES
    n_seq = tm // rows
    keep = CONV_K - 1

    def store_up(c, zr):
        for lt in range(N_LT):
            zbuf_ref[c % ZR_SLOTS, lt, :, rows:, :] = (
                zr[:, lt * LANES:(lt + 1) * LANES].reshape(n_seq, rows, LANES))

    def gate(c):
        slot = c % ZR_SLOTS
        for lt in range(N_LT):
            col = _chunk_col(c, lt)
            zbuf_ref[slot, lt, :, rows - keep:rows, :] = pre_ref[:, :, col:col + LANES]

        def conv(lt):
            col = _chunk_col(c, lt)
            w = cw_ref[:, col:col + LANES]
            z0 = zbuf_ref[slot, lt, :, rows:, :]
            z1 = zbuf_ref[slot, lt, :, rows - 1:2 * rows - 1, :]
            z2 = zbuf_ref[slot, lt, :, rows - 2:2 * rows - 2, :]
            return _conv3(w, z0, z1, z2)

        act = jnp.concatenate(
            [_silu_mul(conv(lt), conv(UP_LT + lt)).reshape(tm, LANES) for lt in range(UP_LT)],
            axis=1)
        act_ref[c % ACT_SLOTS] = act.astype(BF16)
        for lt in range(N_LT):
            col = _chunk_col(c, lt)
            fst_ref[:, :, col:col + LANES] = zbuf_ref[slot, lt, :, 2 * rows - keep:, :]

    _ffn_core(x_ref[...], attn_ref[...], gated_ref[...], wout_ref, g2_ref, wup_ref,
              wdn_ref, gf_ref, y_ref, acc_ref, h2_ref, act_ref, store_up, gate)


def _ffn_weight_specs():
    return [
        _const_spec((D_MODEL, D_MODEL)),
        _const_spec((1, D_MODEL)),
        _const_spec((D_MODEL, 2 * D_FF)),
        _const_spec((CONV_K, 2 * D_FF)),
        _const_spec((D_FF, D_MODEL)),
        _const_spec((1, D_MODEL)),
    ]


def _ffn_prompt(x, attn, gated, wout, g2, wup, cw, wdn, gf):
    b, s, d = x.shape
    tok = lambda bi, ti: (bi, ti, 0)
    return pl.pallas_call(
        _ffn_prompt_kernel,
        grid=(b, s // TM),
        in_specs=[
            pl.BlockSpec((None, TM, d), tok),
            pl.BlockSpec((None, TM, ATT_W), tok),
            pl.BlockSpec((None, TM, CONV_W), tok),
        ] + _ffn_weight_specs(),
        out_specs=(
            pl.BlockSpec((None, TM, d), tok),
            pl.BlockSpec((None, CONV_K - 1, 2 * D_FF), lambda bi, ti: (bi, 0, 0)),
        ),
        out_shape=(
            jax.ShapeDtypeStruct((b, s, d), F32),
            jax.ShapeDtypeStruct((b, CONV_K - 1, 2 * D_FF), F32),
        ),
        scratch_shapes=[
            pltpu.VMEM((NC, N_LT, SUBLANES, LANES), F32),
            pltpu.VMEM((TM, d), F32),
            pltpu.VMEM((TM, d), BF16),
            pltpu.VMEM((ZR_SLOTS, N_LT, SUBLANES + TM, LANES), F32),
            pltpu.VMEM((ACT_SLOTS, TM, FC), BF16),
        ],
        compiler_params=pltpu.CompilerParams(
            dimension_semantics=("arbitrary", "arbitrary"), vmem_limit_bytes=VMEM_LIMIT),
        name="ffn_prompt",
    )(x, attn, gated, wout, g2, wup, cw, wdn, gf)


def _ffn_sample(x, attn, gated, wout, g2, wup, cw, wdn, gf, state):
    n, d = x.shape
    n_seq = TMS // SUBLANES
    tok = lambda ti: (ti, 0)
    seq = lambda ti: (ti, 0, 0)
    return pl.pallas_call(
        _ffn_sample_kernel,
        grid=(n // TMS,),
        in_specs=[
            pl.BlockSpec((TMS, d), tok),
            pl.BlockSpec((TMS, ATT_W), tok),
            pl.BlockSpec((TMS, CONV_W), tok),
        ] + _ffn_weight_specs() + [
            pl.BlockSpec((n_seq, CONV_K - 1, 2 * D_FF), seq),
        ],
        out_specs=(
            pl.BlockSpec((TMS, d), tok),
            pl.BlockSpec((n_seq, CONV_K - 1, 2 * D_FF), seq),
        ),
        out_shape=(
            jax.ShapeDtypeStruct((n, d), F32),
            jax.ShapeDtypeStruct(state.shape, F32),
        ),
        scratch_shapes=[
            pltpu.VMEM((TMS, d), F32),
            pltpu.VMEM((TMS, d), BF16),
            pltpu.VMEM((ZR_SLOTS, N_LT, n_seq, 2 * SUBLANES, LANES), F32),
            pltpu.VMEM((ACT_SLOTS, TMS, FC), BF16),
        ],
        compiler_params=pltpu.CompilerParams(
            dimension_semantics=("arbitrary",), vmem_limit_bytes=VMEM_LIMIT),
        name="ffn_sample",
    )(x, attn, gated, wout, g2, wup, cw, wdn, gf, state)


def _group_prefix(state):
    bsz, k, c = state.shape
    pad = jnp.zeros((bsz, SUBLANES - k, c), state.dtype)
    return jnp.concatenate([state, pad], axis=1).reshape(bsz * SUBLANES, c)


def kernel(x_prompt, x_sample, cache_k, cache_v, state_conv, state_ffn, page_table, norm_mix_g, w_in, conv_w, lambda_q1, lambda_k1, lambda_q2, lambda_k2, subln_g, rel_bias, w_out, norm_ffn_g, w_up, ffn_conv_w, w_down, norm_final_g):
    depth = w_in.shape[0]
    assert depth == 1
    bp, seq, d = x_prompt.shape
    bs, dec, _ = x_sample.shape
    assert dec == SUBLANES and seq % TM == 0 and seq % TA == 0 and (bs * dec) % TMS == 0
    n_pool = cache_k.shape[1]
    layer = 0

    g1 = norm_mix_g[layer].reshape(1, d)
    g2 = norm_ffn_g[layer].reshape(1, d)
    gf = norm_final_g.reshape(1, d)
    w_in_b = w_in[layer].astype(BF16)
    w_out_b = w_out[layer].astype(BF16)
    w_up_b = w_up[layer].astype(BF16)
    w_dn_b = w_down[layer].astype(BF16)
    cw = conv_w[layer]
    fcw = ffn_conv_w[layer]
    lq1 = lambda_q1[layer].reshape(1, HEAD_DIM)
    lk1 = lambda_k1[layer].reshape(1, HEAD_DIM)
    lq2 = lambda_q2[layer].reshape(1, HEAD_DIM)
    lk2 = lambda_k2[layer].reshape(1, HEAD_DIM)
    sg = subln_g[layer].reshape(1, DV)

    q_p, k4_p, v4_p, kb_p, vt_p, gated_p, cst_p = _inproj_prompt(x_prompt, g1, w_in_b, cw)
    attn_p = _attn_prompt(q_p, kb_p, vt_p, rel_bias, lq1, lk1, lq2, lk2, sg)
    y_p, fst_p = _ffn_prompt(x_prompt, attn_p, gated_p, w_out_b, g2, w_up_b, fcw, w_dn_b, gf)

    xs = x_sample.reshape(bs * dec, d)
    q_s, k_s, v_s, gated_s, cu_s = _inproj_sample(xs, g1, w_in_b, cw, _group_prefix(state_conv[layer]))
    ck = cache_k.reshape(depth * n_pool, PAGE_SIZE * H_ATTN, DV)
    cv = cache_v.reshape(depth * n_pool, PAGE_SIZE * H_ATTN, DV)
    attn_s = _attn_sample(page_table, q_s, k_s, v_s, ck, cv, rel_bias, lq1, lk1, lq2, lk2, sg)
    y_s, fst_s = _ffn_sample(xs, attn_s, gated_s, w_out_b, g2, w_up_b, fcw, w_dn_b, gf,
                             state_ffn[layer])

    keep = CONV_K - 1
    y_prompt = y_p
    y_sample = y_s.reshape(bs, dec, d)
    new_k_prompt = k4_p.reshape(1, bp, seq, H_ATTN, DV)
    new_v_prompt = v4_p.reshape(1, bp, seq, H_ATTN, DV)
    new_conv_prompt = cst_p[None]
    new_ffn_prompt = fst_p[None]
    new_k_sample = k_s.reshape(1, bs, dec, H_ATTN, DV)
    new_v_sample = v_s.reshape(1, bs, dec, H_ATTN, DV)
    new_conv_sample = cu_s.reshape(bs, dec, CONV_W)[:, dec - keep:, :][None]
    new_ffn_sample = fst_s[None]
    return (y_prompt, y_sample, new_k_prompt, new_v_prompt, new_conv_prompt, new_ffn_prompt,
            new_k_sample, new_v_sample, new_conv_sample, new_ffn_sample)
```

```python
import functools
import math

import jax
import jax.numpy as jnp
import numpy as np
from jax import lax
from jax.experimental import pallas as pl
from jax.experimental.pallas import tpu as pltpu

D_MODEL = 1024
H_ATTN = 4
HEAD_DIM = 64
DV = 2 * HEAD_DIM
ATT_W = H_ATTN * DV
CONV_W = D_MODEL - ATT_W
CONV_K = 3
D_FF = 2816
NUM_BUCKETS = 32
MAX_EXACT = NUM_BUCKETS // 2
MAX_DISTANCE = 128
PAGE_SIZE = 128
EPS = 1e-6
SCALE = HEAD_DIM ** -0.5
LOG2E = math.log2(math.e)
NEG_INF = -1e30
LAM_INIT = 0.8 - 0.6 * math.exp(-0.3 * 0)

LANES = 128
SUBLANES = 8
VMEM_LIMIT = 56 * 1024 * 1024
VMEM_LIMIT_FUSED = 62 * 1024 * 1024

TM = 512
TMS = 256
FC = 256
NC = D_FF // FC
GATE_ROWS = 64
ZR_SLOTS = 3
ACT_SLOTS = 2
TA = 256
PAGES_PER_STEP = 8
PAGE_SLOTS = 8

F32 = jnp.float32
BF16 = jnp.bfloat16


def _bucket_table(n_max):
    n = np.arange(n_max)
    nf = np.maximum(n, MAX_EXACT).astype(np.float64)
    large = MAX_EXACT + (np.log(nf / MAX_EXACT) / math.log(MAX_DISTANCE / MAX_EXACT)
                         * (NUM_BUCKETS - MAX_EXACT)).astype(np.int64)
    large = np.minimum(large, NUM_BUCKETS - 1)
    return np.where(n < MAX_EXACT, n, large).astype(np.int32)


def _rms(x, g):
    return x * lax.rsqrt(jnp.mean(x * x, axis=-1, keepdims=True) + EPS) * g


def _shift_rows_seq(z, carry):
    row = lax.broadcasted_iota(jnp.int32, z.shape, 0)
    c6 = carry[6:7, :]
    c7 = carry[7:8, :]
    z1 = jnp.where(row == 0, c7, pltpu.roll(z, 1, 0))
    z2 = jnp.where(row == 0, c6, jnp.where(row == 1, c7, pltpu.roll(z, 2, 0)))
    return z1, z2


def _shift_rows_groups(z, prefix):
    row = lax.broadcasted_iota(jnp.int32, z.shape, 0) % SUBLANES
    up_one = pltpu.roll(prefix, prefix.shape[0] - 1, 0)
    z1 = jnp.where(row >= 1, pltpu.roll(z, 1, 0), up_one)
    z2 = jnp.where(row >= 2, pltpu.roll(z, 2, 0), prefix)
    return z1, z2


def _conv3(w, z, z1, z2):
    return w[0:1, :] * z2 + w[1:2, :] * z1 + w[2:3, :] * z


def _project(x, g_ref, w_ref):
    h = _rms(x, g_ref[...]).astype(BF16)
    return jnp.dot(h, w_ref[...], preferred_element_type=F32)


def _store_heads(dst_ref, val):
    rows = val.shape[0]
    for hh in range(H_ATTN):
        dst_ref[pl.ds(hh, rows, stride=H_ATTN), :] = val[:, hh * DV:(hh + 1) * DV]


def _inproj_prompt_kernel(x_ref, g_ref, w_ref, cw_ref, q_ref, k4_ref, v4_ref, kb_ref, vt_ref,
                          gated_ref, cst_ref, carry_ref):
    @pl.when(pl.program_id(1) == 0)
    def _():
        carry_ref[...] = jnp.zeros_like(carry_ref)

    proj = _project(x_ref[...], g_ref, w_ref)
    q = proj[:, 0:ATT_W]
    k = proj[:, ATT_W:2 * ATT_W]
    v = proj[:, 2 * ATT_W:3 * ATT_W]
    gate_b = proj[:, 3 * ATT_W:3 * ATT_W + CONV_W]
    gate_c = proj[:, 3 * ATT_W + CONV_W:3 * ATT_W + 2 * CONV_W]
    u = proj[:, 3 * ATT_W + 2 * CONV_W:]
    q_ref[...] = (q * (SCALE * LOG2E)).astype(BF16)
    kb_ref[...] = k.astype(BF16)
    vt_ref[...] = v.T.astype(BF16)
    _store_heads(k4_ref, k)
    _store_heads(v4_ref, v)
    cu = gate_c * u
    z1, z2 = _shift_rows_seq(cu, carry_ref[...])
    gated_ref[...] = (gate_b * _conv3(cw_ref[...], cu, z1, z2)).astype(BF16)
    rows = cu.shape[0]
    carry_ref[...] = cu[rows - SUBLANES:, :]
    cst_ref[...] = cu[rows - (CONV_K - 1):, :]


def _inproj_sample_kernel(x_ref, g_ref, w_ref, cw_ref, pre_ref, q_ref, k_ref, v_ref, gated_ref,
                          cu_ref):
    proj = _project(x_ref[...], g_ref, w_ref)
    q_ref[...] = proj[:, 0:ATT_W] * SCALE
    k_ref[...] = proj[:, ATT_W:2 * ATT_W]
    v_ref[...] = proj[:, 2 * ATT_W:3 * ATT_W]
    gate_b = proj[:, 3 * ATT_W:3 * ATT_W + CONV_W]
    gate_c = proj[:, 3 * ATT_W + CONV_W:3 * ATT_W + 2 * CONV_W]
    u = proj[:, 3 * ATT_W + 2 * CONV_W:]
    cu = gate_c * u
    z1, z2 = _shift_rows_groups(cu, pre_ref[...])
    gated_ref[...] = (gate_b * _conv3(cw_ref[...], cu, z1, z2)).astype(BF16)
    cu_ref[...] = cu


def _const_spec(shape):
    nd = len(shape)
    return pl.BlockSpec(shape, lambda *_: (0,) * nd, pipeline_mode=pl.Buffered(1))


def _inproj_prompt(x, g, w, cw):
    b, s, d = x.shape
    nt = s // TM
    e = w.shape[1]
    out_shape = (
        jax.ShapeDtypeStruct((b, s, ATT_W), BF16),
        jax.ShapeDtypeStruct((b, s * H_ATTN, DV), F32),
        jax.ShapeDtypeStruct((b, s * H_ATTN, DV), F32),
        jax.ShapeDtypeStruct((b, s, ATT_W), BF16),
        jax.ShapeDtypeStruct((b, ATT_W, s), BF16),
        jax.ShapeDtypeStruct((b, s, CONV_W), BF16),
        jax.ShapeDtypeStruct((b, CONV_K - 1, CONV_W), F32),
    )
    tok = lambda bi, ti: (bi, ti, 0)
    return pl.pallas_call(
        _inproj_prompt_kernel,
        grid=(b, nt),
        in_specs=[
            pl.BlockSpec((None, TM, d), tok),
            _const_spec((1, d)),
            _const_spec((d, e)),
            _const_spec((CONV_K, CONV_W)),
        ],
        out_specs=(
            pl.BlockSpec((None, TM, ATT_W), tok),
            pl.BlockSpec((None, TM * H_ATTN, DV), tok),
            pl.BlockSpec((None, TM * H_ATTN, DV), tok),
            pl.BlockSpec((None, TM, ATT_W), tok),
            pl.BlockSpec((None, ATT_W, TM), lambda bi, ti: (bi, 0, ti)),
            pl.BlockSpec((None, TM, CONV_W), tok),
            pl.BlockSpec((None, CONV_K - 1, CONV_W), lambda bi, ti: (bi, 0, 0)),
        ),
        out_shape=out_shape,
        scratch_shapes=[pltpu.VMEM((SUBLANES, CONV_W), F32)],
        compiler_params=pltpu.CompilerParams(
            dimension_semantics=("arbitrary", "arbitrary"), vmem_limit_bytes=VMEM_LIMIT),
        name="inproj_prompt",
    )(x, g, w, cw)


def _inproj_sample(x, g, w, cw, prefix):
    n, d = x.shape
    e = w.shape[1]
    nt = n // TMS
    tok = lambda ti: (ti, 0)
    out_shape = (
        jax.ShapeDtypeStruct((n, ATT_W), F32),
        jax.ShapeDtypeStruct((n, ATT_W), F32),
        jax.ShapeDtypeStruct((n, ATT_W), F32),
        jax.ShapeDtypeStruct((n, CONV_W), BF16),
        jax.ShapeDtypeStruct((n, CONV_W), F32),
    )
    return pl.pallas_call(
        _inproj_sample_kernel,
        grid=(nt,),
        in_specs=[
            pl.BlockSpec((TMS, d), tok),
            _const_spec((1, d)),
            _const_spec((d, e)),
            _const_spec((CONV_K, CONV_W)),
            pl.BlockSpec((TMS, CONV_W), tok),
        ],
        out_specs=(
            pl.BlockSpec((TMS, ATT_W), tok),
            pl.BlockSpec((TMS, ATT_W), tok),
            pl.BlockSpec((TMS, ATT_W), tok),
            pl.BlockSpec((TMS, CONV_W), tok),
            pl.BlockSpec((TMS, CONV_W), tok),
        ),
        out_shape=out_shape,
        compiler_params=pltpu.CompilerParams(
            dimension_semantics=("arbitrary",), vmem_limit_bytes=VMEM_LIMIT),
        name="inproj_sample",
    )(x, g, w, cw, prefix)


def _lambda(lq1_ref, lk1_ref, lq2_ref, lk2_ref):
    s1 = jnp.sum(lq1_ref[...] * lk1_ref[...], axis=-1, keepdims=True)
    s2 = jnp.sum(lq2_ref[...] * lk2_ref[...], axis=-1, keepdims=True)
    return jnp.exp(s1) - jnp.exp(s2) + LAM_INIT


def _bias_by_code(code, rb_ref, scale=1.0):
    val = jnp.zeros(code.shape, F32)
    for bk in range(NUM_BUCKETS - 1):
        for hh in range(H_ATTN):
            val = jnp.where(code == bk * H_ATTN + hh,
                            (rb_ref[bk, hh] - rb_ref[NUM_BUCKETS - 1, hh]) * scale, val)
    return val


def _attn_prompt_kernel(rb_ref, code_ref, lq1_ref, lk1_ref, lq2_ref, lk2_ref, sg_ref,
                        q_ref, k_ref, vt_ref, o_ref, bias_ref, q2_ref, sa_ref, sb_ref, m_ref, acc_ref):
    t = TA
    nb = t // LANES
    i = pl.program_id(1)

    @pl.when((pl.program_id(0) == 0) & (i == 0))
    def _():
        r = lax.broadcasted_iota(jnp.int32, (LANES, LANES), 0)
        c = lax.broadcasted_iota(jnp.int32, (LANES, LANES), 1)
        base = _bias_by_code(code_ref[...], rb_ref, LOG2E)
        for hh in range(H_ATTN):
            rows = jnp.broadcast_to(base[hh:hh + 1, :], (LANES, LANES))
            circ = pltpu.roll(rows, 0, 1, stride=1, stride_axis=0)
            near = jnp.where(c >= r, circ, NEG_INF)
            wrap = jnp.where(c < r, circ, 0.0)
            zero = jnp.zeros((LANES, LANES), F32)
            for kb in range(nb):
                for qb in range(nb):
                    if qb < kb:
                        blk = jnp.full((LANES, LANES), NEG_INF, F32)
                    elif qb == kb:
                        blk = near
                    elif qb == kb + 1:
                        blk = wrap
                    else:
                        blk = zero
                    bias_ref[hh, 0, kb * LANES:(kb + 1) * LANES, qb * LANES:(qb + 1) * LANES] = blk
                    sub = wrap if (kb == nb - 1 and qb == 0) else zero
                    bias_ref[hh, 1, kb * LANES:(kb + 1) * LANES, qb * LANES:(qb + 1) * LANES] = sub

    lam = _lambda(lq1_ref, lk1_ref, lq2_ref, lk2_ref)
    lane = lax.broadcasted_iota(jnp.int32, (t, DV), 1)
    ones = jnp.ones((2 * SUBLANES, t), BF16)

    for hh in range(H_ATTN):
        qh = q_ref[:, hh * DV:(hh + 1) * DV]
        q2_ref[hh] = jnp.concatenate([jnp.where(lane < HEAD_DIM, qh, jnp.zeros_like(qh)),
                                      jnp.where(lane >= HEAD_DIM, qh, jnp.zeros_like(qh))], axis=0)
    m_ref[...] = jnp.full(m_ref.shape, NEG_INF, F32)
    acc_ref[...] = jnp.zeros_like(acc_ref)

    def logits_t(hh, j):
        start = pl.multiple_of(j * t, t)
        kj = k_ref[pl.ds(start, t), hh * DV:(hh + 1) * DV]
        return lax.dot_general(kj, q2_ref[hh], (((1,), (1,)), ((), ())),
                               preferred_element_type=F32)

    def tile(j, cur_ref, nxt_ref, kind, prefetch):
        start = pl.multiple_of(j * t, t)
        for hh in range(H_ATTN):
            cols = slice(hh * DV, (hh + 1) * DV)
            if prefetch:
                nxt_ref[hh] = logits_t(hh, j + 1)
            st = cur_ref[hh]
            if kind is not None:
                bias = bias_ref[hh, kind]
                st = jnp.concatenate([st[:, :t] + bias, st[:, t:] + bias], axis=1)
            m_old = m_ref[hh]
            m_new = jnp.maximum(m_old, jnp.max(st, axis=0, keepdims=True))
            alpha = jnp.exp2(m_old - m_new)
            p = jnp.exp2(st - m_new).astype(BF16)
            vt = jnp.concatenate([vt_ref[cols, pl.ds(start, t)], ones], axis=0)
            pv = jnp.dot(vt, p, preferred_element_type=F32)
            acc_ref[hh] = alpha * acc_ref[hh] + pv
            m_ref[hh] = m_new

    for hh in range(H_ATTN):
        sa_ref[hh] = logits_t(hh, 0)

    n_pairs = jnp.maximum(i - 1, 0) // 2

    def far_pair(jj, carry):
        tile(2 * jj, sa_ref, sb_ref, None, True)
        tile(2 * jj + 1, sb_ref, sa_ref, None, True)
        return carry

    lax.fori_loop(0, n_pairs, far_pair, 0)

    @pl.when(i == 0)
    def _():
        tile(i, sa_ref, sb_ref, 0, False)

    @pl.when(i % 2 == 1)
    def _():
        tile(i - 1, sa_ref, sb_ref, 1, True)
        tile(i, sb_ref, sa_ref, 0, False)

    @pl.when((i % 2 == 0) & (i >= 2))
    def _():
        tile(i - 2, sa_ref, sb_ref, None, True)
        tile(i - 1, sb_ref, sa_ref, 1, True)
        tile(i, sa_ref, sb_ref, 0, False)

    for hh in range(H_ATTN):
        acc = acc_ref[hh]
        inv_l = 1.0 / acc[DV:DV + 1, :]
        o = acc[0:DV, 0:t] * inv_l[:, 0:t] - lam * (acc[0:DV, t:] * inv_l[:, t:])
        o = o * lax.rsqrt(jnp.mean(o * o, axis=0, keepdims=True) + EPS)
        o_ref[:, hh * DV:(hh + 1) * DV] = ((o.T * sg_ref[...]) * (1.0 - LAM_INIT)).astype(BF16)


def _attn_prompt(q, kb, vt, rel_bias, lq1, lk1, lq2, lk2, sg):
    b, s, _ = q.shape
    t = TA
    code = (_bucket_table(LANES)[None, :] * H_ATTN + np.arange(H_ATTN)[:, None]).astype(np.int32)
    code = np.where(_bucket_table(LANES)[None, :] == NUM_BUCKETS - 1, -1, code)
    smem = pl.BlockSpec(memory_space=pltpu.SMEM)
    small = lambda shape: _const_spec(shape)
    return pl.pallas_call(
        _attn_prompt_kernel,
        grid=(b, s // t),
        in_specs=[
            smem,
            small((H_ATTN, LANES)),
            small((1, HEAD_DIM)), small((1, HEAD_DIM)), small((1, HEAD_DIM)), small((1, HEAD_DIM)),
            small((1, DV)),
            pl.BlockSpec((None, t, ATT_W), lambda bi, qi: (bi, qi, 0)),
            pl.BlockSpec((None, s, ATT_W), lambda bi, qi: (bi, 0, 0)),
            pl.BlockSpec((None, ATT_W, s), lambda bi, qi: (bi, 0, 0)),
        ],
        out_specs=pl.BlockSpec((None, t, ATT_W), lambda bi, qi: (bi, qi, 0)),
        out_shape=jax.ShapeDtypeStruct((b, s, ATT_W), BF16),
        scratch_shapes=[
            pltpu.VMEM((H_ATTN, 2, t, t), F32),
            pltpu.VMEM((H_ATTN, 2 * t, DV), BF16),
            pltpu.VMEM((H_ATTN, t, 2 * t), F32),
            pltpu.VMEM((H_ATTN, t, 2 * t), F32),
            pltpu.VMEM((H_ATTN, 1, 2 * t), F32),
            pltpu.VMEM((H_ATTN, DV + 2 * SUBLANES, 2 * t), F32),
        ],
        compiler_params=pltpu.CompilerParams(
            dimension_semantics=("arbitrary", "arbitrary"), vmem_limit_bytes=VMEM_LIMIT),
        name="attn_prompt",
    )(rel_bias, jnp.asarray(code), lq1, lk1, lq2, lk2, sg, q, kb, vt)


def _page_rows(ref):
    return jnp.concatenate(
        [ref[pl.ds(hh, PAGE_SIZE, stride=H_ATTN), :] for hh in range(H_ATTN)], axis=1)


class _DecodeAttention:
    def __init__(self, seq, n_seq, pt_ref, rb_ref, code_last_ref, code_new_ref,
                 lq1_ref, lk1_ref, lq2_ref, lk2_ref, sg_ref, q_ref, kn_ref, vn_ref,
                 ck_hbm, cv_hbm, o_ref, bias_ref, qbd_ref, kpad_ref, vpad_ref,
                 m_ref, l_ref, acc_ref, kbuf_ref, vbuf_ref, sem_ref):
        self.__dict__.update(locals())
        self.npg = PAGES_PER_STEP
        self.n_phase = pt_ref.shape[1] // self.npg
        self.ahead = PAGE_SLOTS - 1
        self.nq = q_ref.shape[0]
        self.rows = 2 * H_ATTN * self.nq
        self.pending = {}

    def _page_copies(self, seq, phase):
        slot = phase % PAGE_SLOTS
        copies = []
        for pi in range(self.npg):
            page = self.pt_ref[seq, phase * self.npg + pi]
            copies.append(pltpu.make_async_copy(
                self.ck_hbm.at[page], self.kbuf_ref.at[slot, pi], self.sem_ref.at[slot, 0]))
            copies.append(pltpu.make_async_copy(
                self.cv_hbm.at[page], self.vbuf_ref.at[slot, pi], self.sem_ref.at[slot, 1]))
        return copies

    def _start_phase(self, seq, phase):
        for cp in self._page_copies(seq, phase):
            cp.start()

    def _wait_phase(self, seq, phase):
        for cp in self._page_copies(seq, phase):
            cp.wait()

    def _logits_t(self, kmat):
        return lax.dot_general(kmat, self.qbd_ref[...], (((1,), (1,)), ((), ())),
                               preferred_element_type=F32)

    def _softmax(self, s, first):
        m_cur = jnp.max(s, axis=1, keepdims=True)
        if first:
            m_new, alpha = m_cur, None
        else:
            m_old = self.m_ref[...]
            m_new = jnp.maximum(m_old, m_cur)
            alpha = jnp.exp(m_old - m_new)
        p = jnp.exp(s - m_new)
        l_cur = jnp.sum(p, axis=1, keepdims=True)
        self.l_ref[...] = l_cur if first else alpha * self.l_ref[...] + l_cur
        self.m_ref[...] = m_new
        return p, alpha

    def _accumulate(self, p, alpha, v):
        pv = jnp.dot(p, v, preferred_element_type=F32)
        self.acc_ref[...] = pv if alpha is None else alpha * self.acc_ref[...] + pv

    def begin(self):
        nq = self.nq

        @pl.when(self.seq == 0)
        def _():
            for phase in range(self.ahead):
                self._start_phase(0, phase)
            self.bias_ref[0] = _bias_by_code(self.code_last_ref[...], self.rb_ref)
            code_new = self.code_new_ref[...]
            self.bias_ref[1] = jnp.where(code_new == -2, NEG_INF,
                                         _bias_by_code(code_new, self.rb_ref))
            self.kpad_ref[...] = jnp.zeros_like(self.kpad_ref)
            self.vpad_ref[...] = jnp.zeros_like(self.vpad_ref)
            self.qbd_ref[...] = jnp.zeros_like(self.qbd_ref)

        q = self.q_ref[...]
        lane = lax.broadcasted_iota(jnp.int32, q.shape, 1)
        for hh in range(H_ATTN):
            for cc in range(2):
                lo = hh * DV + cc * HEAD_DIM
                r0 = (hh * 2 + cc) * nq
                self.qbd_ref[r0:r0 + nq, :] = jnp.where(
                    (lane >= lo) & (lane < lo + HEAD_DIM), q, 0.0)
        self.kpad_ref[0:nq, :] = self.kn_ref[...]
        self.vpad_ref[0:nq, :] = self.vn_ref[...]
        st = self._logits_t(self.kpad_ref[...]) + self.bias_ref[1]
        p, alpha = self._softmax(st.T[0:self.rows, :], True)
        self._accumulate(p, alpha, self.vpad_ref[...])

    def phase_a(self, phase):
        slot = phase % PAGE_SLOTS
        self._wait_phase(self.seq, phase)
        if phase + self.ahead < self.n_phase:
            self._start_phase(self.seq, phase + self.ahead)
        else:
            @pl.when(self.seq + 1 < self.n_seq)
            def _():
                self._start_phase(self.seq + 1, phase + self.ahead - self.n_phase)
        st_pages = [self._logits_t(_page_rows(self.kbuf_ref.at[slot, pi]))
                    for pi in range(self.npg)]
        if phase == self.n_phase - 1:
            st_pages[-1] = st_pages[-1] + self.bias_ref[0]
        st = jnp.concatenate(st_pages, axis=0)
        self.pending[phase] = self._softmax(st.T[0:self.rows, :], False)

    def phase_b(self, phase):
        slot = phase % PAGE_SLOTS
        p, alpha = self.pending.pop(phase)
        v = jnp.concatenate([_page_rows(self.vbuf_ref.at[slot, pi]) for pi in range(self.npg)],
                            axis=0)
        self._accumulate(p, alpha, v)

    def end(self):
        nq = self.nq
        lam = _lambda(self.lq1_ref, self.lk1_ref, self.lq2_ref, self.lk2_ref)
        o_all = self.acc_ref[...] * (1.0 / self.l_ref[...])
        for hh in range(H_ATTN):
            cols = slice(hh * DV, (hh + 1) * DV)
            r0 = hh * 2 * nq
            o = o_all[r0:r0 + nq, cols] - lam * o_all[r0 + nq:r0 + 2 * nq, cols]
            o = o * lax.rsqrt(jnp.mean(o * o, axis=-1, keepdims=True) + EPS)
            self.o_ref[:, cols] = (o * self.sg_ref[...]) * (1.0 - LAM_INIT)


def _attn_sample_kernel(pt_ref, *refs):
    att = _DecodeAttention(pl.program_id(0), pl.num_programs(0), pt_ref, *refs)
    att.begin()
    for phase in range(att.n_phase):
        att.phase_a(phase)
        att.phase_b(phase)
    att.end()


def _decode_tables(nq):
    rows = 2 * H_ATTN * nq
    bkt = _bucket_table(PAGE_SIZE + nq + 1)
    col = np.arange(LANES)
    col_h = col // (2 * nq)
    col_i = col % nq
    key = np.arange(LANES)[:, None]
    rel_last = PAGE_SIZE + col_i[None, :] - key
    code_last = bkt[rel_last] * H_ATTN + col_h[None, :]
    code_last = np.where((col[None, :] >= rows) | (bkt[rel_last] == NUM_BUCKETS - 1), -1, code_last)
    rel_new = col_i[None, :] - key
    code_new = bkt[np.maximum(rel_new, 0)] * H_ATTN + col_h[None, :]
    code_new = np.where((rel_new < 0) | (key >= nq), -2, code_new)
    code_new = np.where(col[None, :] >= rows, -1, code_new)
    return jnp.asarray(code_last.astype(np.int32)), jnp.asarray(code_new.astype(np.int32))


def _decode_specs(nq, seq_index):
    smem = pl.BlockSpec(memory_space=pltpu.SMEM)
    hbm = pl.BlockSpec(memory_space=pl.ANY)
    small = lambda shape: pl.BlockSpec(shape, lambda *_: (0,) * len(shape))
    per_seq = pl.BlockSpec((nq, ATT_W), lambda *g: (seq_index(*g[:-1]), 0))
    rows = 2 * H_ATTN * nq
    page_buf = pltpu.VMEM((PAGE_SLOTS, PAGES_PER_STEP, PAGE_SIZE * H_ATTN, DV), F32)
    in_specs = [
        smem,
        small((LANES, LANES)), small((LANES, LANES)),
        small((1, HEAD_DIM)), small((1, HEAD_DIM)), small((1, HEAD_DIM)), small((1, HEAD_DIM)),
        small((1, DV)),
        per_seq, per_seq, per_seq,
        hbm, hbm,
    ]
    scratch = [
        pltpu.VMEM((2, LANES, LANES), F32),
        pltpu.VMEM((LANES, ATT_W), F32),
        pltpu.VMEM((LANES, ATT_W), F32),
        pltpu.VMEM((LANES, ATT_W), F32),
        pltpu.VMEM((rows, 1), F32),
        pltpu.VMEM((rows, 1), F32),
        pltpu.VMEM((rows, ATT_W), F32),
        page_buf,
        page_buf,
        pltpu.SemaphoreType.DMA((PAGE_SLOTS, 2)),
    ]
    return in_specs, per_seq, scratch


def _attn_sample(page_table, q, kn, vn, cache_k, cache_v, rel_bias, lq1, lk1, lq2, lk2, sg):
    bs, n_pages = page_table.shape
    nq = q.shape[0] // bs
    assert n_pages % (PAGE_SLOTS * PAGES_PER_STEP) == 0
    code_last, code_new = _decode_tables(nq)
    in_specs, out_spec, scratch = _decode_specs(nq, lambda bi: bi)
    grid_spec = pltpu.PrefetchScalarGridSpec(
        num_scalar_prefetch=1, grid=(bs,), in_specs=in_specs, out_specs=out_spec,
        scratch_shapes=scratch)
    return pl.pallas_call(
        _attn_sample_kernel,
        grid_spec=grid_spec,
        out_shape=jax.ShapeDtypeStruct((bs * nq, ATT_W), F32),
        compiler_params=pltpu.CompilerParams(
            dimension_semantics=("arbitrary",), vmem_limit_bytes=VMEM_LIMIT),
        name="attn_sample",
    )(page_table, rel_bias, code_last, code_new, lq1, lk1, lq2, lk2, sg, q, kn, vn,
      cache_k, cache_v)


N_LT = 2 * FC // LANES
UP_LT = FC // LANES


def _silu_mul(z_up, z_gate):
    return (z_gate * (1.0 / (1.0 + jnp.exp(-z_gate)))) * z_up


def _chunk_col(c, lt):
    return c * FC + lt * LANES if lt < UP_LT else D_FF + c * FC + (lt - UP_LT) * LANES


def _chunk_cols_of(ref, c):
    return jnp.concatenate([ref[:, c * FC:(c + 1) * FC],
                            ref[:, D_FF + c * FC:D_FF + (c + 1) * FC]], axis=1)


def _ffn_core(x, attn, gated, wout_ref, g2_ref, wup_ref, wdn_ref, gf_ref, y_ref,
              acc_ref, h2_ref, act_ref, store_up, gate, side=None):
    phase_of_stage = {}
    if side is not None:
        side.begin()
        phase_of_stage = {p * NC // side.n_phase: p for p in range(side.n_phase)}
        assert len(phase_of_stage) == side.n_phase
    mix = jnp.concatenate([attn.astype(BF16), gated], axis=1)
    x1 = x + jnp.dot(mix, wout_ref[...], preferred_element_type=F32)
    h2_ref[...] = _rms(x1, g2_ref[...]).astype(BF16)
    acc_ref[...] = x1

    def up(c):
        store_up(c, jnp.dot(h2_ref[...], _chunk_cols_of(wup_ref, c), preferred_element_type=F32))

    def down(c):
        acc_ref[...] += jnp.dot(act_ref[c % ACT_SLOTS], wdn_ref[c * FC:(c + 1) * FC, :],
                                preferred_element_type=F32)

    up(0)
    for c in range(NC):
        phase = phase_of_stage.get(c)
        if phase is not None:
            side.phase_a(phase)
        if c + 1 < NC:
            up(c + 1)
        if phase is not None:
            side.phase_b(phase)
        if c >= 1:
            down(c - 1)
        gate(c)
    down(NC - 1)
    if side is not None:
        side.end()
    y_ref[...] = _rms(acc_ref[...], gf_ref[...])


def _ffn_prompt_kernel(x_ref, attn_ref, gated_ref, wout_ref, g2_ref, wup_ref, cw_ref, wdn_ref,
                       gf_ref, y_ref, fst_ref, carry_ref, acc_ref, h2_ref, zbuf_ref, act_ref,
                       side=None):
    @pl.when(pl.program_id(1) == 0)
    def _():
        carry_ref[...] = jnp.zeros_like(carry_ref)

    tm = x_ref.shape[0]
    halo = SUBLANES
    keep = CONV_K - 1

    def store_up(c, zr):
        for lt in range(N_LT):
            zbuf_ref[c % ZR_SLOTS, lt, halo:, :] = zr[:, lt * LANES:(lt + 1) * LANES]

    def gate(c):
        slot = c % ZR_SLOTS
        zbuf_ref[slot, :, 0:halo, :] = carry_ref[c]

        def conv(lt, r):
            col = _chunk_col(c, lt)
            w = cw_ref[:, col:col + LANES]
            z0 = zbuf_ref[slot, lt, pl.ds(halo + r, GATE_ROWS), :]
            z1 = zbuf_ref[slot, lt, pl.ds(halo + r - 1, GATE_ROWS), :]
            z2 = zbuf_ref[slot, lt, pl.ds(halo + r - 2, GATE_ROWS), :]
            return _conv3(w, z0, z1, z2)

        for r in range(0, tm, GATE_ROWS):
            act = jnp.concatenate(
                [_silu_mul(conv(lt, r), conv(UP_LT + lt, r)) for lt in range(UP_LT)], axis=1)
            act_ref[c % ACT_SLOTS, r:r + GATE_ROWS, :] = act.astype(BF16)
        last = zbuf_ref[slot, :, tm:tm + halo, :]
        carry_ref[c] = last
        for lt in range(N_LT):
            col = _chunk_col(c, lt)
            fst_ref[:, col:col + LANES] = last[lt, halo - keep:, :]

    _ffn_core(x_ref[...], attn_ref[...], gated_ref[...], wout_ref, g2_ref, wup_ref,
              wdn_ref, gf_ref, y_ref, acc_ref, h2_ref, act_ref, store_up, gate, side)


N_FFN_IN = 9
N_FFN_OUT = 2
N_FFN_SCRATCH = 5
N_DECODE_IN = 13


def _ffn_prompt_attn_sample_kernel(pt_ref, *refs):
    ffn_in = refs[:N_FFN_IN]
    dec_in = refs[N_FFN_IN:N_FFN_IN + N_DECODE_IN]
    outs = refs[N_FFN_IN + N_DECODE_IN:N_FFN_IN + N_DECODE_IN + N_FFN_OUT + 1]
    scratch = refs[N_FFN_IN + N_DECODE_IN + N_FFN_OUT + 1:]
    seq = pl.program_id(0) * pl.num_programs(1) + pl.program_id(1)
    n_seq = pl.num_programs(0) * pl.num_programs(1)
    side = _DecodeAttention(seq, n_seq, pt_ref, *dec_in, outs[N_FFN_OUT],
                            *scratch[N_FFN_SCRATCH:])
    _ffn_prompt_kernel(*ffn_in, *outs[:N_FFN_OUT], *scratch[:N_FFN_SCRATCH], side=side)


def _ffn_sample_kernel(x_ref, attn_ref, gated_ref, wout_ref, g2_ref, wup_ref, cw_ref, wdn_ref,
                       gf_ref, pre_ref, y_ref, fst_ref, acc_ref, h2_ref, zbuf_ref, act_ref):
    tm = x_ref.shape[0]
    rows = SUBLANES
    n_seq = tm // rows
    keep = CONV_K - 1

    def store_up(c, zr):
        for lt in range(N_LT):
            zbuf_ref[c % ZR_SLOTS, lt, :, rows:, :] = (
                zr[:, lt * LANES:(lt + 1) * LANES].reshape(n_seq, rows, LANES))

    def gate(c):
        slot = c % ZR_SLOTS
        for lt in range(N_LT):
            col = _chunk_col(c, lt)
            zbuf_ref[slot, lt, :, rows - keep:rows, :] = pre_ref[:, :, col:col + LANES]

        def conv(lt):
            col = _chunk_col(c, lt)
            w = cw_ref[:, col:col + LANES]
            z0 = zbuf_ref[slot, lt, :, rows:, :]
            z1 = zbuf_ref[slot, lt, :, rows - 1:2 * rows - 1, :]
            z2 = zbuf_ref[slot, lt, :, rows - 2:2 * rows - 2, :]
            return _conv3(w, z0, z1, z2)

        act = jnp.concatenate(
            [_silu_mul(conv(lt), conv(UP_LT + lt)).reshape(tm, LANES) for lt in range(UP_LT)],
            axis=1)
        act_ref[c % ACT_SLOTS] = act.astype(BF16)
        for lt in range(N_LT):
            col = _chunk_col(c, lt)
            fst_ref[:, :, col:col + LANES] = zbuf_ref[slot, lt, :, 2 * rows - keep:, :]

    _ffn_core(x_ref[...], attn_ref[...], gated_ref[...], wout_ref, g2_ref, wup_ref,
              wdn_ref, gf_ref, y_ref, acc_ref, h2_ref, act_ref, store_up, gate)


def _ffn_weight_specs():
    return [
        _const_spec((D_MODEL, D_MODEL)),
        _const_spec((1, D_MODEL)),
        _const_spec((D_MODEL, 2 * D_FF)),
        _const_spec((CONV_K, 2 * D_FF)),
        _const_spec((D_FF, D_MODEL)),
        _const_spec((1, D_MODEL)),
    ]


def _ffn_prompt_specs(b, s, d):
    tok = lambda bi, ti, *_: (bi, ti, 0)
    in_specs = [
        pl.BlockSpec((None, TM, d), tok),
        pl.BlockSpec((None, TM, ATT_W), tok),
        pl.BlockSpec((None, TM, CONV_W), tok),
    ] + _ffn_weight_specs()
    out_specs = [
        pl.BlockSpec((None, TM, d), tok),
        pl.BlockSpec((None, CONV_K - 1, 2 * D_FF), lambda bi, ti, *_: (bi, 0, 0)),
    ]
    out_shape = [
        jax.ShapeDtypeStruct((b, s, d), F32),
        jax.ShapeDtypeStruct((b, CONV_K - 1, 2 * D_FF), F32),
    ]
    scratch = [
        pltpu.VMEM((NC, N_LT, SUBLANES, LANES), F32),
        pltpu.VMEM((TM, d), F32),
        pltpu.VMEM((TM, d), BF16),
        pltpu.VMEM((ZR_SLOTS, N_LT, SUBLANES + TM, LANES), F32),
        pltpu.VMEM((ACT_SLOTS, TM, FC), BF16),
    ]
    assert (len(in_specs), len(out_specs), len(scratch)) == (N_FFN_IN, N_FFN_OUT, N_FFN_SCRATCH)
    return in_specs, out_specs, out_shape, scratch


def _ffn_prompt(x, attn, gated, wout, g2, wup, cw, wdn, gf):
    b, s, d = x.shape
    in_specs, out_specs, out_shape, scratch = _ffn_prompt_specs(b, s, d)
    return pl.pallas_call(
        _ffn_prompt_kernel,
        grid=(b, s // TM),
        in_specs=in_specs,
        out_specs=out_specs,
        out_shape=out_shape,
        scratch_shapes=scratch,
        compiler_params=pltpu.CompilerParams(
            dimension_semantics=("arbitrary", "arbitrary"), vmem_limit_bytes=VMEM_LIMIT),
        name="ffn_prompt",
    )(x, attn, gated, wout, g2, wup, cw, wdn, gf)


def _ffn_prompt_attn_sample(x, attn, gated, wout, g2, wup, cw, wdn, gf,
                            page_table, q, kn, vn, cache_k, cache_v, rel_bias,
                            lq1, lk1, lq2, lk2, sg):
    b, s, d = x.shape
    nt = s // TM
    bs, n_pages = page_table.shape
    nq = q.shape[0] // bs
    assert bs == b * nt
    assert n_pages % (PAGE_SLOTS * PAGES_PER_STEP) == 0
    code_last, code_new = _decode_tables(nq)
    f_in, f_out, f_shape, f_scratch = _ffn_prompt_specs(b, s, d)
    d_in, d_out, d_scratch = _decode_specs(nq, lambda bi, ti: bi * nt + ti)
    assert len(d_in) == N_DECODE_IN
    grid_spec = pltpu.PrefetchScalarGridSpec(
        num_scalar_prefetch=1, grid=(b, nt), in_specs=f_in + d_in, out_specs=f_out + [d_out],
        scratch_shapes=f_scratch + d_scratch)
    return pl.pallas_call(
        _ffn_prompt_attn_sample_kernel,
        grid_spec=grid_spec,
        out_shape=f_shape + [jax.ShapeDtypeStruct((bs * nq, ATT_W), F32)],
        compiler_params=pltpu.CompilerParams(
            dimension_semantics=("arbitrary", "arbitrary"), vmem_limit_bytes=VMEM_LIMIT_FUSED),
        name="ffn_prompt_attn_sample",
    )(page_table, x, attn, gated, wout, g2, wup, cw, wdn, gf,
      rel_bias, code_last, code_new, lq1, lk1, lq2, lk2, sg, q, kn, vn, cache_k, cache_v)


def _ffn_sample(x, attn, gated, wout, g2, wup, cw, wdn, gf, state):
    n, d = x.shape
    n_seq = TMS // SUBLANES
    tok = lambda ti: (ti, 0)
    seq = lambda ti: (ti, 0, 0)
    return pl.pallas_call(
        _ffn_sample_kernel,
        grid=(n // TMS,),
        in_specs=[
            pl.BlockSpec((TMS, d), tok),
            pl.BlockSpec((TMS, ATT_W), tok),
            pl.BlockSpec((TMS, CONV_W), tok),
        ] + _ffn_weight_specs() + [
            pl.BlockSpec((n_seq, CONV_K - 1, 2 * D_FF), seq),
        ],
        out_specs=(
            pl.BlockSpec((TMS, d), tok),
            pl.BlockSpec((n_seq, CONV_K - 1, 2 * D_FF), seq),
        ),
        out_shape=(
            jax.ShapeDtypeStruct((n, d), F32),
            jax.ShapeDtypeStruct(state.shape, F32),
        ),
        scratch_shapes=[
            pltpu.VMEM((TMS, d), F32),
            pltpu.VMEM((TMS, d), BF16),
            pltpu.VMEM((ZR_SLOTS, N_LT, n_seq, 2 * SUBLANES, LANES), F32),
            pltpu.VMEM((ACT_SLOTS, TMS, FC), BF16),
        ],
        compiler_params=pltpu.CompilerParams(
            dimension_semantics=("arbitrary",), vmem_limit_bytes=VMEM_LIMIT),
        name="ffn_sample",
    )(x, attn, gated, wout, g2, wup, cw, wdn, gf, state)


def _group_prefix(state):
    bsz, k, c = state.shape
    pad = jnp.zeros((bsz, SUBLANES - k, c), state.dtype)
    return jnp.concatenate([state, pad], axis=1).reshape(bsz * SUBLANES, c)


def kernel(x_prompt, x_sample, cache_k, cache_v, state_conv, state_ffn, page_table, norm_mix_g, w_in, conv_w, lambda_q1, lambda_k1, lambda_q2, lambda_k2, subln_g, rel_bias, w_out, norm_ffn_g, w_up, ffn_conv_w, w_down, norm_final_g):
    depth = w_in.shape[0]
    assert depth == 1
    bp, seq, d = x_prompt.shape
    bs, dec, _ = x_sample.shape
    assert dec == SUBLANES and seq % TM == 0 and seq % TA == 0 and (bs * dec) % TMS == 0
    n_pool = cache_k.shape[1]
    layer = 0

    g1 = norm_mix_g[layer].reshape(1, d)
    g2 = norm_ffn_g[layer].reshape(1, d)
    gf = norm_final_g.reshape(1, d)
    w_in_b = w_in[layer].astype(BF16)
    w_out_b = w_out[layer].astype(BF16)
    w_up_b = w_up[layer].astype(BF16)
    w_dn_b = w_down[layer].astype(BF16)
    cw = conv_w[layer]
    fcw = ffn_conv_w[layer]
    lq1 = lambda_q1[layer].reshape(1, HEAD_DIM)
    lk1 = lambda_k1[layer].reshape(1, HEAD_DIM)
    lq2 = lambda_q2[layer].reshape(1, HEAD_DIM)
    lk2 = lambda_k2[layer].reshape(1, HEAD_DIM)
    sg = subln_g[layer].reshape(1, DV)

    q_p, k4_p, v4_p, kb_p, vt_p, gated_p, cst_p = _inproj_prompt(x_prompt, g1, w_in_b, cw)
    attn_p = _attn_prompt(q_p, kb_p, vt_p, rel_bias, lq1, lk1, lq2, lk2, sg)

    xs = x_sample.reshape(bs * dec, d)
    q_s, k_s, v_s, gated_s, cu_s = _inproj_sample(xs, g1, w_in_b, cw, _group_prefix(state_conv[layer]))
    ck = cache_k.reshape(depth * n_pool, PAGE_SIZE * H_ATTN, DV)
    cv = cache_v.reshape(depth * n_pool, PAGE_SIZE * H_ATTN, DV)
    y_p, fst_p = _ffn_prompt(x_prompt, attn_p, gated_p, w_out_b, g2, w_up_b, fcw, w_dn_b, gf)
    attn_s = _attn_sample(page_table, q_s, k_s, v_s, ck, cv, rel_bias, lq1, lk1, lq2, lk2, sg)
    y_s, fst_s = _ffn_sample(xs, attn_s, gated_s, w_out_b, g2, w_up_b, fcw, w_dn_b, gf,
                             state_ffn[layer])

    keep = CONV_K - 1
    y_prompt = y_p
    y_sample = y_s.reshape(bs, dec, d)
    new_k_prompt = k4_p.reshape(1, bp, seq, H_ATTN, DV)
    new_v_prompt = v4_p.reshape(1, bp, seq, H_ATTN, DV)
    new_conv_prompt = cst_p[None]
    new_ffn_prompt = fst_p[None]
    new_k_sample = k_s.reshape(1, bs, dec, H_ATTN, DV)
    new_v_sample = v_s.reshape(1, bs, dec, H_ATTN, DV)
    new_conv_sample = cu_s.reshape(bs, dec, CONV_W)[:, dec - keep:, :][None]
    new_ffn_sample = fst_s[None]
    return (y_prompt, y_sample, new_k_prompt, new_v_prompt, new_conv_prompt, new_ffn_prompt,
            new_k_sample, new_v_sample, new_conv_sample, new_ffn_sample)
```

```python
import functools
import math

import jax
import jax.numpy as jnp
import numpy as np
from jax import lax
from jax.experimental import pallas as pl
from jax.experimental.pallas import tpu as pltpu

D_MODEL = 1024
H_ATTN = 4
HEAD_DIM = 64
DV = 2 * HEAD_DIM
ATT_W = H_ATTN * DV
CONV_W = D_MODEL - ATT_W
CONV_K = 3
D_FF = 2816
NUM_BUCKETS = 32
MAX_EXACT = NUM_BUCKETS // 2
MAX_DISTANCE = 128
PAGE_SIZE = 128
EPS = 1e-6
SCALE = HEAD_DIM ** -0.5
LOG2E = math.log2(math.e)
NEG_INF = -1e30
LAM_INIT = 0.8 - 0.6 * math.exp(-0.3 * 0)

LANES = 128
SUBLANES = 8
VMEM_LIMIT = 56 * 1024 * 1024

TM = 512
TMS = 256
FC = 256
NC = D_FF // FC
GATE_ROWS = 64
ZR_SLOTS = 3
ACT_SLOTS = 2
TA = 256
PAGES_PER_STEP = 8
PAGE_SLOTS = 8

F32 = jnp.float32
BF16 = jnp.bfloat16


def _bucket_table(n_max):
    n = np.arange(n_max)
    nf = np.maximum(n, MAX_EXACT).astype(np.float64)
    large = MAX_EXACT + (np.log(nf / MAX_EXACT) / math.log(MAX_DISTANCE / MAX_EXACT)
                         * (NUM_BUCKETS - MAX_EXACT)).astype(np.int64)
    large = np.minimum(large, NUM_BUCKETS - 1)
    return np.where(n < MAX_EXACT, n, large).astype(np.int32)


def _rms(x, g):
    return x * lax.rsqrt(jnp.mean(x * x, axis=-1, keepdims=True) + EPS) * g


def _shift_rows_seq(z, carry):
    row = lax.broadcasted_iota(jnp.int32, z.shape, 0)
    c6 = carry[6:7, :]
    c7 = carry[7:8, :]
    z1 = jnp.where(row == 0, c7, pltpu.roll(z, 1, 0))
    z2 = jnp.where(row == 0, c6, jnp.where(row == 1, c7, pltpu.roll(z, 2, 0)))
    return z1, z2


def _shift_rows_groups(z, prefix):
    row = lax.broadcasted_iota(jnp.int32, z.shape, 0) % SUBLANES
    up_one = pltpu.roll(prefix, prefix.shape[0] - 1, 0)
    z1 = jnp.where(row >= 1, pltpu.roll(z, 1, 0), up_one)
    z2 = jnp.where(row >= 2, pltpu.roll(z, 2, 0), prefix)
    return z1, z2


def _conv3(w, z, z1, z2):
    return w[0:1, :] * z2 + w[1:2, :] * z1 + w[2:3, :] * z


def _project(x, g_ref, w_ref):
    h = _rms(x, g_ref[...]).astype(BF16)
    return jnp.dot(h, w_ref[...], preferred_element_type=F32)


def _store_heads(dst_ref, val):
    rows = val.shape[0]
    for hh in range(H_ATTN):
        dst_ref[pl.ds(hh, rows, stride=H_ATTN), :] = val[:, hh * DV:(hh + 1) * DV]


def _inproj_prompt_kernel(x_ref, g_ref, w_ref, cw_ref, q_ref, k4_ref, v4_ref, kb_ref, vt_ref,
                          gated_ref, cst_ref, carry_ref):
    @pl.when(pl.program_id(1) == 0)
    def _():
        carry_ref[...] = jnp.zeros_like(carry_ref)

    proj = _project(x_ref[...], g_ref, w_ref)
    q = proj[:, 0:ATT_W]
    k = proj[:, ATT_W:2 * ATT_W]
    v = proj[:, 2 * ATT_W:3 * ATT_W]
    gate_b = proj[:, 3 * ATT_W:3 * ATT_W + CONV_W]
    gate_c = proj[:, 3 * ATT_W + CONV_W:3 * ATT_W + 2 * CONV_W]
    u = proj[:, 3 * ATT_W + 2 * CONV_W:]
    q_ref[...] = (q * (SCALE * LOG2E)).astype(BF16)
    kb_ref[...] = k.astype(BF16)
    vt_ref[...] = v.T.astype(BF16)
    _store_heads(k4_ref, k)
    _store_heads(v4_ref, v)
    cu = gate_c * u
    z1, z2 = _shift_rows_seq(cu, carry_ref[...])
    gated_ref[...] = (gate_b * _conv3(cw_ref[...], cu, z1, z2)).astype(BF16)
    rows = cu.shape[0]
    carry_ref[...] = cu[rows - SUBLANES:, :]
    cst_ref[...] = cu[rows - (CONV_K - 1):, :]


def _inproj_sample_kernel(x_ref, g_ref, w_ref, cw_ref, pre_ref, q_ref, k_ref, v_ref, gated_ref,
                          cu_ref):
    proj = _project(x_ref[...], g_ref, w_ref)
    q_ref[...] = proj[:, 0:ATT_W] * SCALE
    k_ref[...] = proj[:, ATT_W:2 * ATT_W]
    v_ref[...] = proj[:, 2 * ATT_W:3 * ATT_W]
    gate_b = proj[:, 3 * ATT_W:3 * ATT_W + CONV_W]
    gate_c = proj[:, 3 * ATT_W + CONV_W:3 * ATT_W + 2 * CONV_W]
    u = proj[:, 3 * ATT_W + 2 * CONV_W:]
    cu = gate_c * u
    z1, z2 = _shift_rows_groups(cu, pre_ref[...])
    gated_ref[...] = (gate_b * _conv3(cw_ref[...], cu, z1, z2)).astype(BF16)
    cu_ref[...] = cu


def _const_spec(shape):
    nd = len(shape)
    return pl.BlockSpec(shape, lambda *_: (0,) * nd, pipeline_mode=pl.Buffered(1))


def _inproj_prompt(x, g, w, cw):
    b, s, d = x.shape
    nt = s // TM
    e = w.shape[1]
    out_shape = (
        jax.ShapeDtypeStruct((b, s, ATT_W), BF16),
        jax.ShapeDtypeStruct((b, s * H_ATTN, DV), F32),
        jax.ShapeDtypeStruct((b, s * H_ATTN, DV), F32),
        jax.ShapeDtypeStruct((b, s, ATT_W), BF16),
        jax.ShapeDtypeStruct((b, ATT_W, s), BF16),
        jax.ShapeDtypeStruct((b, s, CONV_W), BF16),
        jax.ShapeDtypeStruct((b, CONV_K - 1, CONV_W), F32),
    )
    tok = lambda bi, ti: (bi, ti, 0)
    return pl.pallas_call(
        _inproj_prompt_kernel,
        grid=(b, nt),
        in_specs=[
            pl.BlockSpec((None, TM, d), tok),
            _const_spec((1, d)),
            _const_spec((d, e)),
            _const_spec((CONV_K, CONV_W)),
        ],
        out_specs=(
            pl.BlockSpec((None, TM, ATT_W), tok),
            pl.BlockSpec((None, TM * H_ATTN, DV), tok),
            pl.BlockSpec((None, TM * H_ATTN, DV), tok),
            pl.BlockSpec((None, TM, ATT_W), tok),
            pl.BlockSpec((None, ATT_W, TM), lambda bi, ti: (bi, 0, ti)),
            pl.BlockSpec((None, TM, CONV_W), tok),
            pl.BlockSpec((None, CONV_K - 1, CONV_W), lambda bi, ti: (bi, 0, 0)),
        ),
        out_shape=out_shape,
        scratch_shapes=[pltpu.VMEM((SUBLANES, CONV_W), F32)],
        compiler_params=pltpu.CompilerParams(
            dimension_semantics=("arbitrary", "arbitrary"), vmem_limit_bytes=VMEM_LIMIT),
        name="inproj_prompt",
    )(x, g, w, cw)


def _inproj_sample(x, g, w, cw, prefix):
    n, d = x.shape
    e = w.shape[1]
    nt = n // TMS
    tok = lambda ti: (ti, 0)
    out_shape = (
        jax.ShapeDtypeStruct((n, ATT_W), F32),
        jax.ShapeDtypeStruct((n, ATT_W), F32),
        jax.ShapeDtypeStruct((n, ATT_W), F32),
        jax.ShapeDtypeStruct((n, CONV_W), BF16),
        jax.ShapeDtypeStruct((n, CONV_W), F32),
    )
    return pl.pallas_call(
        _inproj_sample_kernel,
        grid=(nt,),
        in_specs=[
            pl.BlockSpec((TMS, d), tok),
            _const_spec((1, d)),
            _const_spec((d, e)),
            _const_spec((CONV_K, CONV_W)),
            pl.BlockSpec((TMS, CONV_W), tok),
        ],
        out_specs=(
            pl.BlockSpec((TMS, ATT_W), tok),
            pl.BlockSpec((TMS, ATT_W), tok),
            pl.BlockSpec((TMS, ATT_W), tok),
            pl.BlockSpec((TMS, CONV_W), tok),
            pl.BlockSpec((TMS, CONV_W), tok),
        ),
        out_shape=out_shape,
        compiler_params=pltpu.CompilerParams(
            dimension_semantics=("arbitrary",), vmem_limit_bytes=VMEM_LIMIT),
        name="inproj_sample",
    )(x, g, w, cw, prefix)


def _lambda(lq1_ref, lk1_ref, lq2_ref, lk2_ref):
    s1 = jnp.sum(lq1_ref[...] * lk1_ref[...], axis=-1, keepdims=True)
    s2 = jnp.sum(lq2_ref[...] * lk2_ref[...], axis=-1, keepdims=True)
    return jnp.exp(s1) - jnp.exp(s2) + LAM_INIT


def _bias_by_code(code, rb_ref, scale=1.0):
    val = jnp.zeros(code.shape, F32)
    for bk in range(NUM_BUCKETS - 1):
        for hh in range(H_ATTN):
            val = jnp.where(code == bk * H_ATTN + hh,
                            (rb_ref[bk, hh] - rb_ref[NUM_BUCKETS - 1, hh]) * scale, val)
    return val


def _attn_prompt_kernel(rb_ref, code_ref, lq1_ref, lk1_ref, lq2_ref, lk2_ref, sg_ref,
                        q_ref, k_ref, vt_ref, o_ref, bias_ref, q2_ref, sa_ref, sb_ref, m_ref, acc_ref):
    t = TA
    nb = t // LANES
    i = pl.program_id(1)

    @pl.when((pl.program_id(0) == 0) & (i == 0))
    def _():
        r = lax.broadcasted_iota(jnp.int32, (LANES, LANES), 0)
        c = lax.broadcasted_iota(jnp.int32, (LANES, LANES), 1)
        base = _bias_by_code(code_ref[...], rb_ref, LOG2E)
        for hh in range(H_ATTN):
            rows = jnp.broadcast_to(base[hh:hh + 1, :], (LANES, LANES))
            circ = pltpu.roll(rows, 0, 1, stride=1, stride_axis=0)
            near = jnp.where(c >= r, circ, NEG_INF)
            wrap = jnp.where(c < r, circ, 0.0)
            zero = jnp.zeros((LANES, LANES), F32)
            for kb in range(nb):
                for qb in range(nb):
                    if qb < kb:
                        blk = jnp.full((LANES, LANES), NEG_INF, F32)
                    elif qb == kb:
                        blk = near
                    elif qb == kb + 1:
                        blk = wrap
                    else:
                        blk = zero
                    bias_ref[hh, 0, kb * LANES:(kb + 1) * LANES, qb * LANES:(qb + 1) * LANES] = blk
                    sub = wrap if (kb == nb - 1 and qb == 0) else zero
                    bias_ref[hh, 1, kb * LANES:(kb + 1) * LANES, qb * LANES:(qb + 1) * LANES] = sub

    lam = _lambda(lq1_ref, lk1_ref, lq2_ref, lk2_ref)
    lane = lax.broadcasted_iota(jnp.int32, (t, DV), 1)
    ones = jnp.ones((2 * SUBLANES, t), BF16)

    for hh in range(H_ATTN):
        qh = q_ref[:, hh * DV:(hh + 1) * DV]
        q2_ref[hh] = jnp.concatenate([jnp.where(lane < HEAD_DIM, qh, jnp.zeros_like(qh)),
                                      jnp.where(lane >= HEAD_DIM, qh, jnp.zeros_like(qh))], axis=0)
    m_ref[...] = jnp.full(m_ref.shape, NEG_INF, F32)
    acc_ref[...] = jnp.zeros_like(acc_ref)

    def logits_t(hh, j):
        start = pl.multiple_of(j * t, t)
        kj = k_ref[pl.ds(start, t), hh * DV:(hh + 1) * DV]
        return lax.dot_general(kj, q2_ref[hh], (((1,), (1,)), ((), ())),
                               preferred_element_type=F32)

    def tile(j, cur_ref, nxt_ref, kind, prefetch):
        start = pl.multiple_of(j * t, t)
        for hh in range(H_ATTN):
            cols = slice(hh * DV, (hh + 1) * DV)
            if prefetch:
                nxt_ref[hh] = logits_t(hh, j + 1)
            st = cur_ref[hh]
            if kind is not None:
                bias = bias_ref[hh, kind]
                st = jnp.concatenate([st[:, :t] + bias, st[:, t:] + bias], axis=1)
            m_old = m_ref[hh]
            m_new = jnp.maximum(m_old, jnp.max(st, axis=0, keepdims=True))
            alpha = jnp.exp2(m_old - m_new)
            p = jnp.exp2(st - m_new).astype(BF16)
            vt = jnp.concatenate([vt_ref[cols, pl.ds(start, t)], ones], axis=0)
            pv = jnp.dot(vt, p, preferred_element_type=F32)
            acc_ref[hh] = alpha * acc_ref[hh] + pv
            m_ref[hh] = m_new

    for hh in range(H_ATTN):
        sa_ref[hh] = logits_t(hh, 0)

    n_pairs = jnp.maximum(i - 1, 0) // 2

    def far_pair(jj, carry):
        tile(2 * jj, sa_ref, sb_ref, None, True)
        tile(2 * jj + 1, sb_ref, sa_ref, None, True)
        return carry

    lax.fori_loop(0, n_pairs, far_pair, 0)

    @pl.when(i == 0)
    def _():
        tile(i, sa_ref, sb_ref, 0, False)

    @pl.when(i % 2 == 1)
    def _():
        tile(i - 1, sa_ref, sb_ref, 1, True)
        tile(i, sb_ref, sa_ref, 0, False)

    @pl.when((i % 2 == 0) & (i >= 2))
    def _():
        tile(i - 2, sa_ref, sb_ref, None, True)
        tile(i - 1, sb_ref, sa_ref, 1, True)
        tile(i, sa_ref, sb_ref, 0, False)

    for hh in range(H_ATTN):
        acc = acc_ref[hh]
        inv_l = 1.0 / acc[DV:DV + 1, :]
        o = acc[0:DV, 0:t] * inv_l[:, 0:t] - lam * (acc[0:DV, t:] * inv_l[:, t:])
        o = o * lax.rsqrt(jnp.mean(o * o, axis=0, keepdims=True) + EPS)
        o_ref[:, hh * DV:(hh + 1) * DV] = ((o.T * sg_ref[...]) * (1.0 - LAM_INIT)).astype(BF16)


def _attn_prompt(q, kb, vt, rel_bias, lq1, lk1, lq2, lk2, sg):
    b, s, _ = q.shape
    t = TA
    code = (_bucket_table(LANES)[None, :] * H_ATTN + np.arange(H_ATTN)[:, None]).astype(np.int32)
    code = np.where(_bucket_table(LANES)[None, :] == NUM_BUCKETS - 1, -1, code)
    smem = pl.BlockSpec(memory_space=pltpu.SMEM)
    small = lambda shape: _const_spec(shape)
    return pl.pallas_call(
        _attn_prompt_kernel,
        grid=(b, s // t),
        in_specs=[
            smem,
            small((H_ATTN, LANES)),
            small((1, HEAD_DIM)), small((1, HEAD_DIM)), small((1, HEAD_DIM)), small((1, HEAD_DIM)),
            small((1, DV)),
            pl.BlockSpec((None, t, ATT_W), lambda bi, qi: (bi, qi, 0)),
            pl.BlockSpec((None, s, ATT_W), lambda bi, qi: (bi, 0, 0)),
            pl.BlockSpec((None, ATT_W, s), lambda bi, qi: (bi, 0, 0)),
        ],
        out_specs=pl.BlockSpec((None, t, ATT_W), lambda bi, qi: (bi, qi, 0)),
        out_shape=jax.ShapeDtypeStruct((b, s, ATT_W), BF16),
        scratch_shapes=[
            pltpu.VMEM((H_ATTN, 2, t, t), F32),
            pltpu.VMEM((H_ATTN, 2 * t, DV), BF16),
            pltpu.VMEM((H_ATTN, t, 2 * t), F32),
            pltpu.VMEM((H_ATTN, t, 2 * t), F32),
            pltpu.VMEM((H_ATTN, 1, 2 * t), F32),
            pltpu.VMEM((H_ATTN, DV + 2 * SUBLANES, 2 * t), F32),
        ],
        compiler_params=pltpu.CompilerParams(
            dimension_semantics=("arbitrary", "arbitrary"), vmem_limit_bytes=VMEM_LIMIT),
        name="attn_prompt",
    )(rel_bias, jnp.asarray(code), lq1, lk1, lq2, lk2, sg, q, kb, vt)


def _page_rows(ref):
    return jnp.concatenate(
        [ref[pl.ds(hh, PAGE_SIZE, stride=H_ATTN), :] for hh in range(H_ATTN)], axis=1)


class _DecodeAttention:
    def __init__(self, seq, n_seq, pt_ref, rb_ref, code_last_ref, code_new_ref,
                 lq1_ref, lk1_ref, lq2_ref, lk2_ref, sg_ref, q_ref, kn_ref, vn_ref,
                 ck_hbm, cv_hbm, o_ref, bias_ref, qbd_ref, kpad_ref, vpad_ref,
                 m_ref, l_ref, acc_ref, kbuf_ref, vbuf_ref, sem_ref):
        self.__dict__.update(locals())
        self.npg = PAGES_PER_STEP
        self.n_phase = pt_ref.shape[1] // self.npg
        self.ahead = PAGE_SLOTS - 1
        self.nq = q_ref.shape[0]
        self.rows = 2 * H_ATTN * self.nq
        self.pending = {}

    def _page_copies(self, seq, phase):
        slot = phase % PAGE_SLOTS
        copies = []
        for pi in range(self.npg):
            page = self.pt_ref[seq, phase * self.npg + pi]
            copies.append(pltpu.make_async_copy(
                self.ck_hbm.at[page], self.kbuf_ref.at[slot, pi], self.sem_ref.at[slot, 0]))
            copies.append(pltpu.make_async_copy(
                self.cv_hbm.at[page], self.vbuf_ref.at[slot, pi], self.sem_ref.at[slot, 1]))
        return copies

    def _start_phase(self, seq, phase):
        for cp in self._page_copies(seq, phase):
            cp.start()

    def _wait_phase(self, seq, phase):
        for cp in self._page_copies(seq, phase):
            cp.wait()

    def _logits_t(self, kmat):
        return lax.dot_general(kmat, self.qbd_ref[...], (((1,), (1,)), ((), ())),
                               preferred_element_type=F32)

    def _softmax(self, s, first):
        m_cur = jnp.max(s, axis=1, keepdims=True)
        if first:
            m_new, alpha = m_cur, None
        else:
            m_old = self.m_ref[...]
            m_new = jnp.maximum(m_old, m_cur)
            alpha = jnp.exp(m_old - m_new)
        p = jnp.exp(s - m_new)
        l_cur = jnp.sum(p, axis=1, keepdims=True)
        self.l_ref[...] = l_cur if first else alpha * self.l_ref[...] + l_cur
        self.m_ref[...] = m_new
        return p, alpha

    def _accumulate(self, p, alpha, v):
        pv = jnp.dot(p, v, preferred_element_type=F32)
        self.acc_ref[...] = pv if alpha is None else alpha * self.acc_ref[...] + pv

    def begin(self):
        nq = self.nq

        @pl.when(self.seq == 0)
        def _():
            for phase in range(self.ahead):
                self._start_phase(0, phase)
            self.bias_ref[0] = _bias_by_code(self.code_last_ref[...], self.rb_ref)
            code_new = self.code_new_ref[...]
            self.bias_ref[1] = jnp.where(code_new == -2, NEG_INF,
                                         _bias_by_code(code_new, self.rb_ref))
            self.kpad_ref[...] = jnp.zeros_like(self.kpad_ref)
            self.vpad_ref[...] = jnp.zeros_like(self.vpad_ref)
            self.qbd_ref[...] = jnp.zeros_like(self.qbd_ref)

        q = self.q_ref[...]
        lane = lax.broadcasted_iota(jnp.int32, q.shape, 1)
        for hh in range(H_ATTN):
            for cc in range(2):
                lo = hh * DV + cc * HEAD_DIM
                r0 = (hh * 2 + cc) * nq
                self.qbd_ref[r0:r0 + nq, :] = jnp.where(
                    (lane >= lo) & (lane < lo + HEAD_DIM), q, 0.0)
        self.kpad_ref[0:nq, :] = self.kn_ref[...]
        self.vpad_ref[0:nq, :] = self.vn_ref[...]
        st = self._logits_t(self.kpad_ref[...]) + self.bias_ref[1]
        p, alpha = self._softmax(st.T[0:self.rows, :], True)
        self._accumulate(p, alpha, self.vpad_ref[...])

    def phase_a(self, phase):
        slot = phase % PAGE_SLOTS
        self._wait_phase(self.seq, phase)
        if phase + self.ahead < self.n_phase:
            self._start_phase(self.seq, phase + self.ahead)
        else:
            @pl.when(self.seq + 1 < self.n_seq)
            def _():
                self._start_phase(self.seq + 1, phase + self.ahead - self.n_phase)
        st_pages = [self._logits_t(_page_rows(self.kbuf_ref.at[slot, pi]))
                    for pi in range(self.npg)]
        if phase == self.n_phase - 1:
            st_pages[-1] = st_pages[-1] + self.bias_ref[0]
        st = jnp.concatenate(st_pages, axis=0)
        self.pending[phase] = self._softmax(st.T[0:self.rows, :], False)

    def phase_b(self, phase):
        slot = phase % PAGE_SLOTS
        p, alpha = self.pending.pop(phase)
        v = jnp.concatenate([_page_rows(self.vbuf_ref.at[slot, pi]) for pi in range(self.npg)],
                            axis=0)
        self._accumulate(p, alpha, v)

    def end(self):
        nq = self.nq
        lam = _lambda(self.lq1_ref, self.lk1_ref, self.lq2_ref, self.lk2_ref)
        o_all = self.acc_ref[...] * (1.0 / self.l_ref[...])
        for hh in range(H_ATTN):
            cols = slice(hh * DV, (hh + 1) * DV)
            r0 = hh * 2 * nq
            o = o_all[r0:r0 + nq, cols] - lam * o_all[r0 + nq:r0 + 2 * nq, cols]
            o = o * lax.rsqrt(jnp.mean(o * o, axis=-1, keepdims=True) + EPS)
            self.o_ref[:, cols] = (o * self.sg_ref[...]) * (1.0 - LAM_INIT)


def _attn_sample_kernel(pt_ref, *refs):
    att = _DecodeAttention(pl.program_id(0), pl.num_programs(0), pt_ref, *refs)
    att.begin()
    for phase in range(att.n_phase):
        att.phase_a(phase)
        att.phase_b(phase)
    att.end()


def _decode_tables(nq):
    rows = 2 * H_ATTN * nq
    bkt = _bucket_table(PAGE_SIZE + nq + 1)
    col = np.arange(LANES)
    col_h = col // (2 * nq)
    col_i = col % nq
    key = np.arange(LANES)[:, None]
    rel_last = PAGE_SIZE + col_i[None, :] - key
    code_last = bkt[rel_last] * H_ATTN + col_h[None, :]
    code_last = np.where((col[None, :] >= rows) | (bkt[rel_last] == NUM_BUCKETS - 1), -1, code_last)
    rel_new = col_i[None, :] - key
    code_new = bkt[np.maximum(rel_new, 0)] * H_ATTN + col_h[None, :]
    code_new = np.where((rel_new < 0) | (key >= nq), -2, code_new)
    code_new = np.where(col[None, :] >= rows, -1, code_new)
    return jnp.asarray(code_last.astype(np.int32)), jnp.asarray(code_new.astype(np.int32))


def _decode_specs(nq, seq_index):
    smem = pl.BlockSpec(memory_space=pltpu.SMEM)
    hbm = pl.BlockSpec(memory_space=pl.ANY)
    small = lambda shape: pl.BlockSpec(shape, lambda *_: (0,) * len(shape))
    per_seq = pl.BlockSpec((nq, ATT_W), lambda *g: (seq_index(*g[:-1]), 0))
    rows = 2 * H_ATTN * nq
    page_buf = pltpu.VMEM((PAGE_SLOTS, PAGES_PER_STEP, PAGE_SIZE * H_ATTN, DV), F32)
    in_specs = [
        smem,
        small((LANES, LANES)), small((LANES, LANES)),
        small((1, HEAD_DIM)), small((1, HEAD_DIM)), small((1, HEAD_DIM)), small((1, HEAD_DIM)),
        small((1, DV)),
        per_seq, per_seq, per_seq,
        hbm, hbm,
    ]
    scratch = [
        pltpu.VMEM((2, LANES, LANES), F32),
        pltpu.VMEM((LANES, ATT_W), F32),
        pltpu.VMEM((LANES, ATT_W), F32),
        pltpu.VMEM((LANES, ATT_W), F32),
        pltpu.VMEM((rows, 1), F32),
        pltpu.VMEM((rows, 1), F32),
        pltpu.VMEM((rows, ATT_W), F32),
        page_buf,
        page_buf,
        pltpu.SemaphoreType.DMA((PAGE_SLOTS, 2)),
    ]
    return in_specs, per_seq, scratch


def _attn_sample(page_table, q, kn, vn, cache_k, cache_v, rel_bias, lq1, lk1, lq2, lk2, sg):
    bs, n_pages = page_table.shape
    nq = q.shape[0] // bs
    assert n_pages % (PAGE_SLOTS * PAGES_PER_STEP) == 0
    code_last, code_new = _decode_tables(nq)
    in_specs, out_spec, scratch = _decode_specs(nq, lambda bi: bi)
    grid_spec = pltpu.PrefetchScalarGridSpec(
        num_scalar_prefetch=1, grid=(bs,), in_specs=in_specs, out_specs=out_spec,
        scratch_shapes=scratch)
    return pl.pallas_call(
        _attn_sample_kernel,
        grid_spec=grid_spec,
        out_shape=jax.ShapeDtypeStruct((bs * nq, ATT_W), F32),
        compiler_params=pltpu.CompilerParams(
            dimension_semantics=("arbitrary",), vmem_limit_bytes=VMEM_LIMIT),
        name="attn_sample",
    )(page_table, rel_bias, code_last, code_new, lq1, lk1, lq2, lk2, sg, q, kn, vn,
      cache_k, cache_v)


N_LT = 2 * FC // LANES
UP_LT = FC // LANES


def _silu_mul(z_up, z_gate):
    return (z_gate * (1.0 / (1.0 + jnp.exp(-z_gate)))) * z_up


def _chunk_col(c, lt):
    return c * FC + lt * LANES if lt < UP_LT else D_FF + c * FC + (lt - UP_LT) * LANES


def _chunk_cols_of(ref, c):
    return jnp.concatenate([ref[:, c * FC:(c + 1) * FC],
                            ref[:, D_FF + c * FC:D_FF + (c + 1) * FC]], axis=1)


def _ffn_core(x, attn, gated, wout_ref, g2_ref, wup_ref, wdn_ref, gf_ref, y_ref,
              acc_ref, h2_ref, act_ref, store_up, gate):
    mix = jnp.concatenate([attn.astype(BF16), gated], axis=1)
    x1 = x + jnp.dot(mix, wout_ref[...], preferred_element_type=F32)
    h2_ref[...] = _rms(x1, g2_ref[...]).astype(BF16)
    acc_ref[...] = x1

    def up(c):
        store_up(c, jnp.dot(h2_ref[...], _chunk_cols_of(wup_ref, c), preferred_element_type=F32))

    def down(c):
        acc_ref[...] += jnp.dot(act_ref[c % ACT_SLOTS], wdn_ref[c * FC:(c + 1) * FC, :],
                                preferred_element_type=F32)

    up(0)
    for c in range(NC):
        if c + 1 < NC:
            up(c + 1)
        if c >= 1:
            down(c - 1)
        gate(c)
    down(NC - 1)
    y_ref[...] = _rms(acc_ref[...], gf_ref[...])


def _ffn_prompt_kernel(x_ref, attn_ref, gated_ref, wout_ref, g2_ref, wup_ref, cw_ref, wdn_ref,
                       gf_ref, y_ref, fst_ref, carry_ref, acc_ref, h2_ref, zbuf_ref, act_ref):
    @pl.when(pl.program_id(1) == 0)
    def _():
        carry_ref[...] = jnp.zeros_like(carry_ref)

    tm = x_ref.shape[0]
    halo = SUBLANES
    keep = CONV_K - 1

    def store_up(c, zr):
        for lt in range(N_LT):
            zbuf_ref[c % ZR_SLOTS, lt, halo:, :] = zr[:, lt * LANES:(lt + 1) * LANES]

    def gate(c):
        slot = c % ZR_SLOTS
        zbuf_ref[slot, :, 0:halo, :] = carry_ref[c]

        def conv(lt, r):
            col = _chunk_col(c, lt)
            w = cw_ref[:, col:col + LANES]
            z0 = zbuf_ref[slot, lt, pl.ds(halo + r, GATE_ROWS), :]
            z1 = zbuf_ref[slot, lt, pl.ds(halo + r - 1, GATE_ROWS), :]
            z2 = zbuf_ref[slot, lt, pl.ds(halo + r - 2, GATE_ROWS), :]
            return _conv3(w, z0, z1, z2)

        for r in range(0, tm, GATE_ROWS):
            act = jnp.concatenate(
                [_silu_mul(conv(lt, r), conv(UP_LT + lt, r)) for lt in range(UP_LT)], axis=1)
            act_ref[c % ACT_SLOTS, r:r + GATE_ROWS, :] = act.astype(BF16)
        last = zbuf_ref[slot, :, tm:tm + halo, :]
        carry_ref[c] = last
        for lt in range(N_LT):
            col = _chunk_col(c, lt)
            fst_ref[:, col:col + LANES] = last[lt, halo - keep:, :]

    _ffn_core(x_ref[...], attn_ref[...], gated_ref[...], wout_ref, g2_ref, wup_ref,
              wdn_ref, gf_ref, y_ref, acc_ref, h2_ref, act_ref, store_up, gate)


def _ffn_sample_kernel(x_ref, attn_ref, gated_ref, wout_ref, g2_ref, wup_ref, cw_ref, wdn_ref,
                       gf_ref, pre_ref, y_ref, fst_ref, acc_ref, h2_ref, zbuf_ref, act_ref):
    tm = x_ref.shape[0]
    rows = SUBLANES
    n_seq = tm // rows
    keep = CONV_K - 1

    def store_up(c, zr):
        for lt in range(N_LT):
            zbuf_ref[c % ZR_SLOTS, lt, :, rows:, :] = (
                zr[:, lt * LANES:(lt + 1) * LANES].reshape(n_seq, rows, LANES))

    def gate(c):
        slot = c % ZR_SLOTS
        for lt in range(N_LT):
            col = _chunk_col(c, lt)
            zbuf_ref[slot, lt, :, rows - keep:rows, :] = pre_ref[:, :, col:col + LANES]

        def conv(lt):
            col = _chunk_col(c, lt)
            w = cw_ref[:, col:col + LANES]
            z0 = zbuf_ref[slot, lt, :, rows:, :]
            z1 = zbuf_ref[slot, lt, :, rows - 1:2 * rows - 1, :]
            z2 = zbuf_ref[slot, lt, :, rows - 2:2 * rows - 2, :]
            return _conv3(w, z0, z1, z2)

        act = jnp.concatenate(
            [_silu_mul(conv(lt), conv(UP_LT + lt)).reshape(tm, LANES) for lt in range(UP_LT)],
            axis=1)
        act_ref[c % ACT_SLOTS] = act.astype(BF16)
        for lt in range(N_LT):
            col = _chunk_col(c, lt)
            fst_ref[:, :, col:col + LANES] = zbuf_ref[slot, lt, :, 2 * rows - keep:, :]

    _ffn_core(x_ref[...], attn_ref[...], gated_ref[...], wout_ref, g2_ref, wup_ref,
              wdn_ref, gf_ref, y_ref, acc_ref, h2_ref, act_ref, store_up, gate)


def _ffn_weight_specs():
    return [
        _const_spec((D_MODEL, D_MODEL)),
        _const_spec((1, D_MODEL)),
        _const_spec((D_MODEL, 2 * D_FF)),
        _const_spec((CONV_K, 2 * D_FF)),
        _const_spec((D_FF, D_MODEL)),
        _const_spec((1, D_MODEL)),
    ]


def _ffn_prompt_specs(b, s, d):
    tok = lambda bi, ti, *_: (bi, ti, 0)
    in_specs = [
        pl.BlockSpec((None, TM, d), tok),
        pl.BlockSpec((None, TM, ATT_W), tok),
        pl.BlockSpec((None, TM, CONV_W), tok),
    ] + _ffn_weight_specs()
    out_specs = [
        pl.BlockSpec((None, TM, d), tok),
        pl.BlockSpec((None, CONV_K - 1, 2 * D_FF), lambda bi, ti, *_: (bi, 0, 0)),
    ]
    out_shape = [
        jax.ShapeDtypeStruct((b, s, d), F32),
        jax.ShapeDtypeStruct((b, CONV_K - 1, 2 * D_FF), F32),
    ]
    scratch = [
        pltpu.VMEM((NC, N_LT, SUBLANES, LANES), F32),
        pltpu.VMEM((TM, d), F32),
        pltpu.VMEM((TM, d), BF16),
        pltpu.VMEM((ZR_SLOTS, N_LT, SUBLANES + TM, LANES), F32),
        pltpu.VMEM((ACT_SLOTS, TM, FC), BF16),
    ]
    return in_specs, out_specs, out_shape, scratch


def _ffn_prompt(x, attn, gated, wout, g2, wup, cw, wdn, gf):
    b, s, d = x.shape
    in_specs, out_specs, out_shape, scratch = _ffn_prompt_specs(b, s, d)
    return pl.pallas_call(
        _ffn_prompt_kernel,
        grid=(b, s // TM),
        in_specs=in_specs,
        out_specs=out_specs,
        out_shape=out_shape,
        scratch_shapes=scratch,
        compiler_params=pltpu.CompilerParams(
            dimension_semantics=("arbitrary", "arbitrary"), vmem_limit_bytes=VMEM_LIMIT),
        name="ffn_prompt",
    )(x, attn, gated, wout, g2, wup, cw, wdn, gf)


def _ffn_sample(x, attn, gated, wout, g2, wup, cw, wdn, gf, state):
    n, d = x.shape
    n_seq = TMS // SUBLANES
    tok = lambda ti: (ti, 0)
    seq = lambda ti: (ti, 0, 0)
    return pl.pallas_call(
        _ffn_sample_kernel,
        grid=(n // TMS,),
        in_specs=[
            pl.BlockSpec((TMS, d), tok),
            pl.BlockSpec((TMS, ATT_W), tok),
            pl.BlockSpec((TMS, CONV_W), tok),
        ] + _ffn_weight_specs() + [
            pl.BlockSpec((n_seq, CONV_K - 1, 2 * D_FF), seq),
        ],
        out_specs=(
            pl.BlockSpec((TMS, d), tok),
            pl.BlockSpec((n_seq, CONV_K - 1, 2 * D_FF), seq),
        ),
        out_shape=(
            jax.ShapeDtypeStruct((n, d), F32),
            jax.ShapeDtypeStruct(state.shape, F32),
        ),
        scratch_shapes=[
            pltpu.VMEM((TMS, d), F32),
            pltpu.VMEM((TMS, d), BF16),
            pltpu.VMEM((ZR_SLOTS, N_LT, n_seq, 2 * SUBLANES, LANES), F32),
            pltpu.VMEM((ACT_SLOTS, TMS, FC), BF16),
        ],
        compiler_params=pltpu.CompilerParams(
            dimension_semantics=("arbitrary",), vmem_limit_bytes=VMEM_LIMIT),
        name="ffn_sample",
    )(x, attn, gated, wout, g2, wup, cw, wdn, gf, state)


def _group_prefix(state):
    bsz, k, c = state.shape
    pad = jnp.zeros((bsz, SUBLANES - k, c), state.dtype)
    return jnp.concatenate([state, pad], axis=1).reshape(bsz * SUBLANES, c)


def kernel(x_prompt, x_sample, cache_k, cache_v, state_conv, state_ffn, page_table, norm_mix_g, w_in, conv_w, lambda_q1, lambda_k1, lambda_q2, lambda_k2, subln_g, rel_bias, w_out, norm_ffn_g, w_up, ffn_conv_w, w_down, norm_final_g):
    depth = w_in.shape[0]
    assert depth == 1
    bp, seq, d = x_prompt.shape
    bs, dec, _ = x_sample.shape
    assert dec == SUBLANES and seq % TM == 0 and seq % TA == 0 and (bs * dec) % TMS == 0
    n_pool = cache_k.shape[1]
    layer = 0

    g1 = norm_mix_g[layer].reshape(1, d)
    g2 = norm_ffn_g[layer].reshape(1, d)
    gf = norm_final_g.reshape(1, d)
    w_in_b = w_in[layer].astype(BF16)
    w_out_b = w_out[layer].astype(BF16)
    w_up_b = w_up[layer].astype(BF16)
    w_dn_b = w_down[layer].astype(BF16)
    cw = conv_w[layer]
    fcw = ffn_conv_w[layer]
    lq1 = lambda_q1[layer].reshape(1, HEAD_DIM)
    lk1 = lambda_k1[layer].reshape(1, HEAD_DIM)
    lq2 = lambda_q2[layer].reshape(1, HEAD_DIM)
    lk2 = lambda_k2[layer].reshape(1, HEAD_DIM)
    sg = subln_g[layer].reshape(1, DV)

    q_p, k4_p, v4_p, kb_p, vt_p, gated_p, cst_p = _inproj_prompt(x_prompt, g1, w_in_b, cw)
    attn_p = _attn_prompt(q_p, kb_p, vt_p, rel_bias, lq1, lk1, lq2, lk2, sg)

    xs = x_sample.reshape(bs * dec, d)
    q_s, k_s, v_s, gated_s, cu_s = _inproj_sample(xs, g1, w_in_b, cw, _group_prefix(state_conv[layer]))
    ck = cache_k.reshape(depth * n_pool, PAGE_SIZE * H_ATTN, DV)
    cv = cache_v.reshape(depth * n_pool, PAGE_SIZE * H_ATTN, DV)
    y_p, fst_p = _ffn_prompt(x_prompt, attn_p, gated_p, w_out_b, g2, w_up_b, fcw, w_dn_b, gf)
    attn_s = _attn_sample(page_table, q_s, k_s, v_s, ck, cv, rel_bias, lq1, lk1, lq2, lk2, sg)
    y_s, fst_s = _ffn_sample(xs, attn_s, gated_s, w_out_b, g2, w_up_b, fcw, w_dn_b, gf,
                             state_ffn[layer])

    keep = CONV_K - 1
    y_prompt = y_p
    y_sample = y_s.reshape(bs, dec, d)
    new_k_prompt = k4_p.reshape(1, bp, seq, H_ATTN, DV)
    new_v_prompt = v4_p.reshape(1, bp, seq, H_ATTN, DV)
    new_conv_prompt = cst_p[None]
    new_ffn_prompt = fst_p[None]
    new_k_sample = k_s.reshape(1, bs, dec, H_ATTN, DV)
    new_v_sample = v_s.reshape(1, bs, dec, H_ATTN, DV)
    new_conv_sample = cu_s.reshape(bs, dec, CONV_W)[:, dec - keep:, :][None]
    new_ffn_sample = fst_s[None]
    return (y_prompt, y_sample, new_k_prompt, new_v_prompt, new_conv_prompt, new_ffn_prompt,
            new_k_sample, new_v_sample, new_conv_sample, new_ffn_sample)
```
